```python
import math
import jax, jax.numpy as jnp
from jax import lax
import numpy as np

D_MODEL = 2048
BATCH = 2
SEQ = 4096
DEPTH = 4
DEC_BATCH = 8
DEC_SEQ = 1
PAST_LEN = 16384
PAGE_SIZE = 128

N_A_LAYERS = DEPTH // 2
N_B_LAYERS = DEPTH - N_A_LAYERS
A_HEADS = 8
A_DQK = D_MODEL // (2 * A_HEADS)
A_DV = D_MODEL // A_HEADS
A_CHUNK = 128
B_HEADS = 16
B_HD = D_MODEL // B_HEADS
MOBA_BLOCK = 256
MOBA_TOPK = 3
Q_BLOCK = 32
D_FF = ((8 * D_MODEL // 3 + 127) // 128) * 128
CONV_W = 3
EPS = 1e-6
NEG = -1e30

kernel_name = "yoco_mlstm_moba_convffn_step"


def rmsnorm(x, g):
    xf = x.astype(jnp.float32)
    y = xf * lax.rsqrt(jnp.mean(xf * xf, axis=-1, keepdims=True) + EPS)
    return (y * g.astype(jnp.float32)).astype(x.dtype)


def alibi_slopes():
    h = jnp.arange(1, B_HEADS + 1, dtype=jnp.float32)
    return jnp.exp2(-8.0 * h / B_HEADS)


def mlstm_chunk(carry, inp):
    C, n, m = carry
    q, k, v, ig, lf = inp
    L = q.shape[2]
    b = jnp.cumsum(lf, axis=-1)
    causal = jnp.tril(jnp.ones((L, L), dtype=bool))
    logD = jnp.where(causal, b[..., :, None] - b[..., None, :] + ig[..., None, :], NEG)
    m_t = jnp.maximum(b + m[..., None], logD.max(axis=-1))
    D = jnp.exp(logD - m_t[..., None])
    inter = jnp.exp(b + m[..., None] - m_t)
    W = jnp.einsum('bhtd,bhsd->bhts', q, k) * D
    num = inter[..., None] * jnp.einsum('bhtd,bhdv->bhtv', q, C) + jnp.einsum('bhts,bhsv->bhtv', W, v)
    den = inter * jnp.einsum('bhtd,bhd->bht', q, n) + W.sum(axis=-1)
    h = num / jnp.maximum(jnp.abs(den), jnp.exp(-m_t))[..., None]
    m_new = m_t[..., -1]
    w_s = jnp.exp(b[..., -1:] - b + ig - m_new[..., None])
    decay = jnp.exp(b[..., -1] + m - m_new)
    C_new = decay[..., None, None] * C + jnp.einsum('bhs,bhsd,bhsv->bhdv', w_s, k, v)
    n_new = decay[..., None] * n + jnp.einsum('bhs,bhsd->bhd', w_s, k)
    return (C_new, n_new, m_new), h


def mlstm_mixer(xn, w_in, b_if, head_g, w_out, C0, n0, m0, chunk):
    B, T, _ = xn.shape
    H, dk, dv = A_HEADS, A_DQK, A_DV
    f32 = jnp.float32
    proj = xn @ w_in
    s1 = H * dk
    s2 = 2 * s1
    s3 = s2 + H * dv
    s4 = s3 + H * dv

    def heads(a, d):
        return a.reshape(B, T, H, d).transpose(0, 2, 1, 3).astype(f32)

    q = heads(proj[..., :s1], dk) * (dk ** -0.5)
    k = heads(proj[..., s1:s2], dk)
    v = heads(proj[..., s2:s3], dv)
    o_gate = jax.nn.sigmoid(proj[..., s3:s4].astype(f32))
    gates = proj[..., s4:].astype(f32) + b_if.astype(f32)
    ig = gates[..., :H].transpose(0, 2, 1)
    lf = jax.nn.log_sigmoid(gates[..., H:]).transpose(0, 2, 1)
    nc = T // chunk

    def chunks(a):
        return jnp.moveaxis(a.reshape(a.shape[:2] + (nc, chunk) + a.shape[3:]), 2, 0)

    (C, n, m), h = lax.scan(mlstm_chunk, (C0.astype(f32), n0.astype(f32), m0.astype(f32)),
                            (chunks(q), chunks(k), chunks(v), chunks(ig), chunks(lf)))
    h = jnp.moveaxis(h, 0, 2).reshape(B, H, T, dv).transpose(0, 2, 1, 3)
    h = h * lax.rsqrt(jnp.mean(h * h, axis=-1, keepdims=True) + EPS)
    h = h.reshape(B, T, H * dv) * head_g.astype(f32) * o_gate
    return h.astype(xn.dtype) @ w_out, (C, n, m)


def conv_ffn(xn, w_up, conv_w, conv_b, w_down, buf):
    up = xn @ w_up
    u, g = up[..., :D_FF], up[..., D_FF:]
    T = g.shape[1]
    gpad = jnp.concatenate([buf.astype(g.dtype), g], axis=1)
    gc = conv_b + sum(gpad[:, j:j + T] * conv_w[j] for j in range(CONV_W))
    y = (jax.nn.silu(gc) * u) @ w_down
    return y, gpad[:, -(CONV_W - 1):]


def shared_kv(x, g_kv, w_kv, g_k):
    B, T, _ = x.shape
    kv = rmsnorm(x, g_kv) @ w_kv
    k = rmsnorm(kv[..., :B_HEADS * B_HD].reshape(B, T, B_HEADS, B_HD), g_k)
    v = kv[..., B_HEADS * B_HD:].reshape(B, T, B_HEADS, B_HD)
    return k, v


def kv_blocks(k_pieces, v_pieces):
    B = k_pieces[0].shape[0]
    L = sum(p.shape[1] for p in k_pieces)
    nb = -(-L // MOBA_BLOCK)
    pad = nb * MOBA_BLOCK - L
    z = jnp.zeros((B, pad, B_HEADS, B_HD), k_pieces[0].dtype)
    kb = jnp.concatenate(list(k_pieces) + [z], axis=1).reshape(B, nb, MOBA_BLOCK, B_HEADS, B_HD)
    vb = jnp.concatenate(list(v_pieces) + [z.astype(v_pieces[0].dtype)], axis=1).reshape(B, nb, MOBA_BLOCK, B_HEADS, B_HD)
    kmean = kb.astype(jnp.float32).mean(axis=2)
    return kb, vb, kmean


def moba_block(qc, pos, kb, vb, kmean, slopes):
    B, nb = kmean.shape[:2]
    H = B_HEADS
    qb = qc.shape[1]
    topk = min(MOBA_TOPK, nb)
    qf = qc.astype(jnp.float32)
    own = pos // MOBA_BLOCK
    gate = jnp.einsum('bqhd,bnhd->bhqn', qf, kmean)
    past = jnp.arange(nb)[None, :] < own[:, None]
    gate = jnp.where(past, gate, NEG)
    _, top_i = lax.top_k(gate, topk)
    valid = top_i < own[:, None]
    idx = jnp.concatenate([top_i, jnp.broadcast_to(own[:, None], (B, H, qb, 1))], axis=-1)
    valid = jnp.concatenate([valid, jnp.ones((B, H, qb, 1), dtype=bool)], axis=-1)
    bi = jnp.arange(B)[:, None, None, None]
    hi = jnp.arange(H)[None, :, None, None]
    k_sel = kb[bi, idx, :, hi].astype(jnp.float32)
    v_sel = vb[bi, idx, :, hi].astype(jnp.float32)
    kpos = idx[..., None] * MOBA_BLOCK + jnp.arange(MOBA_BLOCK)
    dist = pos[:, None, None] - kpos
    s = jnp.einsum('bqhd,bhqjsd->bhqjs', qf, k_sel) - slopes[:, None, None, None] * dist
    s = jnp.where((dist >= 0) & valid[..., None], s, NEG)
    J = topk + 1
    p = jax.nn.softmax(s.reshape(B, H, qb, J * MOBA_BLOCK), axis=-1).reshape(B, H, qb, J, MOBA_BLOCK)
    o = jnp.einsum('bhqjs,bhqjsd->bqhd', p, v_sel)
    return o.astype(qc.dtype)


def moba_mixer(xn, w_q, g_q, w_o, kb, vb, kmean, pos, q_block, slopes):
    B, T, _ = xn.shape
    q = rmsnorm((xn @ w_q).reshape(B, T, B_HEADS, B_HD), g_q) * (B_HD ** -0.5)
    nq = T // q_block
    qc = q.reshape(B, nq, q_block, B_HEADS, B_HD).transpose(1, 0, 2, 3, 4)
    pc = pos.reshape(nq, q_block)
    o = lax.map(lambda a: moba_block(a[0], a[1], kb, vb, kmean, slopes), (qc, pc))
    o = o.transpose(1, 0, 2, 3, 4).reshape(B, T, B_HEADS * B_HD)
    return o @ w_o


def setup_inputs(seed: int = 0) -> dict:
    key = jax.random.key(seed)
    ks = jax.random.split(key, 32)
    f32 = jnp.float32

    def nrm(k, shape, scale):
        return jax.random.normal(k, shape, f32) * scale

    n_pages = PAST_LEN // PAGE_SIZE
    used = DEC_BATCH * n_pages
    n_phys = used + max(1, used // 4)
    a_cols = 2 * A_HEADS * A_DQK + 2 * A_HEADS * A_DV + 2 * A_HEADS
    page_table = jax.random.permutation(ks[0], n_phys)[:used].reshape(DEC_BATCH, n_pages).astype(jnp.int32)
    return {
        "x_prompt": nrm(ks[1], (BATCH, SEQ, D_MODEL), 1.0),
        "x_sample": nrm(ks[2], (DEC_BATCH, DEC_SEQ, D_MODEL), 1.0),
        "state_mlstm_C": nrm(ks[3], (N_A_LAYERS, DEC_BATCH, A_HEADS, A_DQK, A_DV), 0.5),
        "state_mlstm_n": nrm(ks[4], (N_A_LAYERS, DEC_BATCH, A_HEADS, A_DQK), 0.5),
        "state_mlstm_m": nrm(ks[5], (N_A_LAYERS, DEC_BATCH, A_HEADS), 1.0),
        "state_conv": nrm(ks[6], (DEPTH, DEC_BATCH, CONV_W - 1, D_FF), 1.0),
        "cache_k": nrm(ks[7], (n_phys, PAGE_SIZE, B_HEADS, B_HD), 1.0),
        "cache_v": nrm(ks[8], (n_phys, PAGE_SIZE, B_HEADS, B_HD), 1.0),
        "page_table": page_table,
        "a_norm_g": 1.0 + nrm(ks[9], (N_A_LAYERS, D_MODEL), 0.01),
        "a_w_in": nrm(ks[10], (N_A_LAYERS, D_MODEL, a_cols), D_MODEL ** -0.5),
        "a_b_if": jnp.concatenate([nrm(ks[11], (N_A_LAYERS, A_HEADS), 0.1),
                                   3.0 + nrm(ks[12], (N_A_LAYERS, A_HEADS), 0.1)], axis=-1),
        "a_head_g": 1.0 + nrm(ks[13], (N_A_LAYERS, A_HEADS * A_DV), 0.01),
        "a_w_out": nrm(ks[14], (N_A_LAYERS, A_HEADS * A_DV, D_MODEL), (A_HEADS * A_DV) ** -0.5),
        "kv_norm_g": 1.0 + nrm(ks[15], (D_MODEL,), 0.01),
        "w_kv": nrm(ks[16], (D_MODEL, 2 * B_HEADS * B_HD), D_MODEL ** -0.5),
        "k_norm_g": 1.0 + nrm(ks[17], (B_HD,), 0.01),
        "b_norm_g": 1.0 + nrm(ks[18], (N_B_LAYERS, D_MODEL), 0.01),
        "b_w_q": nrm(ks[19], (N_B_LAYERS, D_MODEL, B_HEADS * B_HD), D_MODEL ** -0.5),
        "q_norm_g": 1.0 + nrm(ks[20], (N_B_LAYERS, B_HD), 0.01),
        "b_w_o": nrm(ks[21], (N_B_LAYERS, B_HEADS * B_HD, D_MODEL), (B_HEADS * B_HD) ** -0.5),
        "f_norm_g": 1.0 + nrm(ks[22], (DEPTH, D_MODEL), 0.01),
        "f_w_up": nrm(ks[23], (DEPTH, D_MODEL, 2 * D_FF), D_MODEL ** -0.5),
        "f_conv_w": nrm(ks[24], (DEPTH, CONV_W, D_FF), CONV_W ** -0.5),
        "f_conv_b": nrm(ks[25], (DEPTH, D_FF), 0.02),
        "f_w_down": nrm(ks[26], (DEPTH, D_FF, D_MODEL), D_FF ** -0.5),
    }


def reference(x_prompt, x_sample, state_mlstm_C, state_mlstm_n, state_mlstm_m, state_conv, cache_k, cache_v, page_table,
              a_norm_g, a_w_in, a_b_if, a_head_g, a_w_out,
              kv_norm_g, w_kv, k_norm_g,
              b_norm_g, b_w_q, q_norm_g, b_w_o,
              f_norm_g, f_w_up, f_conv_w, f_conv_b, f_w_down):
    f32 = jnp.float32
    slopes = alibi_slopes()
    n_pages = PAST_LEN // PAGE_SIZE
    pos_p = jnp.arange(SEQ, dtype=jnp.int32)
    pos_s = PAST_LEN + jnp.arange(DEC_SEQ, dtype=jnp.int32)
    xp, xs = x_prompt, x_sample
    Cp, np_, mp, Cs, ns, ms = [], [], [], [], [], []
    conv_p, conv_s = [], []
    k_p = v_p = k_s = v_s = None
    blocks_p = blocks_s = None
    for layer in range(DEPTH):
        if layer < N_A_LAYERS:
            a = layer
            zC = jnp.zeros((BATCH, A_HEADS, A_DQK, A_DV), f32)
            zn = jnp.zeros((BATCH, A_HEADS, A_DQK), f32)
            zm = jnp.zeros((BATCH, A_HEADS), f32)
            hp, (c1, n1, m1) = mlstm_mixer(rmsnorm(xp, a_norm_g[a]), a_w_in[a], a_b_if[a], a_head_g[a], a_w_out[a],
                                           zC, zn, zm, math.gcd(A_CHUNK, SEQ))
            hs, (c2, n2, m2) = mlstm_mixer(rmsnorm(xs, a_norm_g[a]), a_w_in[a], a_b_if[a], a_head_g[a], a_w_out[a],
                                           state_mlstm_C[a], state_mlstm_n[a], state_mlstm_m[a], DEC_SEQ)
            Cp.append(c1); np_.append(n1); mp.append(m1)
            Cs.append(c2); ns.append(n2); ms.append(m2)
        else:
            b = layer - N_A_LAYERS
            if b == 0:
                k_p, v_p = shared_kv(xp, kv_norm_g, w_kv, k_norm_g)
                k_s, v_s = shared_kv(xs, kv_norm_g, w_kv, k_norm_g)
                blocks_p = kv_blocks([k_p], [v_p])
                k_past = cache_k[page_table].reshape(DEC_BATCH, n_pages * PAGE_SIZE, B_HEADS, B_HD)
                v_past = cache_v[page_table].reshape(DEC_BATCH, n_pages * PAGE_SIZE, B_HEADS, B_HD)
                blocks_s = kv_blocks([k_past.astype(k_s.dtype), k_s], [v_past.astype(v_s.dtype), v_s])
            hp = moba_mixer(rmsnorm(xp, b_norm_g[b]), b_w_q[b], q_norm_g[b], b_w_o[b], *blocks_p, pos_p,
                            math.gcd(Q_BLOCK, SEQ), slopes)
            hs = moba_mixer(rmsnorm(xs, b_norm_g[b]), b_w_q[b], q_norm_g[b], b_w_o[b], *blocks_s, pos_s,
                            DEC_SEQ, slopes)
        xp = xp + hp
        xs = xs + hs
        zbuf = jnp.zeros((BATCH, CONV_W - 1, D_FF), xp.dtype)
        fp, bp = conv_ffn(rmsnorm(xp, f_norm_g[layer]), f_w_up[layer], f_conv_w[layer], f_conv_b[layer], f_w_down[layer], zbuf)
        fs, bs = conv_ffn(rmsnorm(xs, f_norm_g[layer]), f_w_up[layer], f_conv_w[layer], f_conv_b[layer], f_w_down[layer], state_conv[layer])
        xp = xp + fp
        xs = xs + fs
        conv_p.append(bp); conv_s.append(bs)
    return (xp, xs,
            jnp.stack(Cp), jnp.stack(np_), jnp.stack(mp),
            jnp.stack(Cs), jnp.stack(ns), jnp.stack(ms),
            jnp.stack(conv_p), jnp.stack(conv_s),
            k_p, v_p, k_s, v_s)
```

```python
import functools
import math

import jax
import jax.numpy as jnp
from jax import lax
from jax.experimental import pallas as pl
from jax.experimental.pallas import tpu as pltpu

F32 = jnp.float32
BF16 = jnp.bfloat16

EPS = 1e-6
NEG = -1e30
LANES = 128
SUBLANES = 8
MXU_N = 256
VMEM_LIMIT = 56 * 1024 * 1024

A_HEADS = 8
A_CHUNK = 128
B_HEADS = 16
MOBA_BLOCK = 256
MOBA_TOPK = 3
CONV_W = 3


def _cparams(sem):
    return pltpu.CompilerParams(dimension_semantics=sem, vmem_limit_bytes=VMEM_LIMIT)


def _row_tile(m):
    return min(m, 1024)


def _col_tile(n):
    for t in (512, 256, 128):
        if n % t == 0:
            return t
    return n


def _norm_matmul_body(*refs, n_norm_tiles, n_tiles, head_scale, has_side):
    it = iter(refs)
    x_ref, g_ref, w_ref = next(it), next(it), next(it)
    hg_ref = next(it) if n_norm_tiles else None
    w2_ref = next(it) if has_side else None
    o_ref = next(it)
    o2_ref = next(it) if has_side else None
    xn_ref = next(it)
    j = pl.program_id(1)

    @pl.when(j == 0)
    def _():
        x = x_ref[...]
        ms = jnp.mean(x * x, axis=-1, keepdims=True)
        xn = (x * lax.rsqrt(ms + EPS) * g_ref[...]).astype(BF16)
        xn_ref[...] = xn
        if has_side:
            o2_ref[...] = jnp.dot(xn, w2_ref[...], preferred_element_type=F32)

    acc = jnp.dot(xn_ref[...], w_ref[...], preferred_element_type=F32)

    def headnorm(a):
        parts = []
        for c in range(a.shape[1] // LANES):
            p = a[:, c * LANES:(c + 1) * LANES]
            ms = jnp.mean(p * p, axis=-1, keepdims=True)
            p = p * lax.rsqrt(ms + EPS) * hg_ref[...]
            if head_scale != 1.0:
                p = p * head_scale
            parts.append(p)
        return jnp.concatenate(parts, axis=1) if len(parts) > 1 else parts[0]

    if n_norm_tiles == 0:
        o_ref[...] = acc.astype(o_ref.dtype)
    elif n_norm_tiles >= n_tiles:
        o_ref[...] = headnorm(acc).astype(o_ref.dtype)
    else:
        @pl.when(j < n_norm_tiles)
        def _():
            o_ref[...] = headnorm(acc).astype(o_ref.dtype)

        @pl.when(j >= n_norm_tiles)
        def _():
            o_ref[...] = acc.astype(o_ref.dtype)


def norm_matmul(x, g, w, *, head_g=None, n_norm_cols=0, head_scale=1.0, w_side=None, name):
    m, k = x.shape
    n = w.shape[1]
    tm, tn = _row_tile(m), _col_tile(n)
    n_tiles = n // tn
    assert n_norm_cols % tn == 0
    n_norm_tiles = n_norm_cols // tn
    in_specs = [pl.BlockSpec((tm, k), lambda i, j: (i, 0)),
                pl.BlockSpec((1, k), lambda i, j: (0, 0)),
                pl.BlockSpec((k, tn), lambda i, j: (0, j))]
    args = [x, g.reshape(1, k), w]
    if n_norm_tiles:
        in_specs.append(pl.BlockSpec((1, LANES), lambda i, j: (0, 0)))
        args.append(head_g.reshape(1, LANES))
    out_shape = [jax.ShapeDtypeStruct((m, n), F32)]
    out_specs = [pl.BlockSpec((tm, tn), lambda i, j: (i, j))]
    if w_side is not None:
        in_specs.append(pl.BlockSpec((k, LANES), lambda i, j: (0, 0)))
        args.append(w_side)
        out_shape.append(jax.ShapeDtypeStruct((m, LANES), F32))
        out_specs.append(pl.BlockSpec((tm, LANES), lambda i, j: (i, 0)))
    body = functools.partial(_norm_matmul_body, n_norm_tiles=n_norm_tiles, n_tiles=n_tiles,
                             head_scale=head_scale, has_side=w_side is not None)
    outs = pl.pallas_call(
        body, grid=(m // tm, n_tiles), in_specs=in_specs, out_specs=out_specs, out_shape=out_shape,
        scratch_shapes=[pltpu.VMEM((tm, k), BF16)],
        compiler_params=_cparams(("parallel", "arbitrary")), name=name)(*args)
    return outs if w_side is not None else outs[0]


def _matmul_res_body(a_ref, w_ref, r_ref, o_ref):
    o_ref[...] = r_ref[...] + jnp.dot(a_ref[...], w_ref[...], preferred_element_type=F32)


def matmul_res(a, w, res, *, name):
    m, k = a.shape
    n = w.shape[1]
    tm, tn = _row_tile(m), min(_col_tile(n), MXU_N)
    return pl.pallas_call(
        _matmul_res_body, grid=(m // tm, n // tn),
        in_specs=[pl.BlockSpec((tm, k), lambda i, j: (i, 0)),
                  pl.BlockSpec((k, tn), lambda i, j: (0, j)),
                  pl.BlockSpec((tm, tn), lambda i, j: (i, j))],
        out_specs=pl.BlockSpec((tm, tn), lambda i, j: (i, j)),
        out_shape=jax.ShapeDtypeStruct((m, n), F32),
        compiler_params=_cparams(("parallel", "arbitrary")), name=name)(a, w, res)


def _ffn_up_body(*refs, seq_mode, blocks_per_seq):
    if seq_mode:
        x_ref, xh_ref, g_ref, w_ref, cw_ref, cb_ref, act_ref, gt_ref, xn_ref, xhn_ref = refs
    else:
        x_ref, p2_ref, p1_ref, g_ref, w_ref, cw_ref, cb_ref, act_ref, gt_ref, xn_ref = refs
    i, j = pl.program_id(0), pl.program_id(1)

    def norm(x):
        ms = jnp.mean(x * x, axis=-1, keepdims=True)
        return (x * lax.rsqrt(ms + EPS) * g_ref[...]).astype(BF16)

    @pl.when(j == 0)
    def _():
        xn_ref[...] = norm(x_ref[...])
        if seq_mode:
            xhn_ref[...] = norm(xh_ref[...])

    acc = jnp.dot(xn_ref[...], w_ref[...], preferred_element_type=F32)
    u, g = acc[:, :LANES], acc[:, LANES:]
    tm = g.shape[0]
    if seq_mode:
        gh = jnp.dot(xhn_ref[...], w_ref[...], preferred_element_type=F32)[:, LANES:]
        gh = jnp.where(i % blocks_per_seq == 0, 0.0, gh)
        row = lax.broadcasted_iota(jnp.int32, g.shape, 0)
        prev1, prev2 = gh[SUBLANES - 1:SUBLANES, :], gh[SUBLANES - 2:SUBLANES - 1, :]
        g1 = jnp.where(row == 0, prev1, pltpu.roll(g, 1, 0))
        g2 = jnp.where(row == 0, prev2, jnp.where(row == 1, prev1, pltpu.roll(g, 2, 0)))
        gt_ref[0] = g[tm - SUBLANES:, :]
    else:
        g1, g2 = p1_ref[...], p2_ref[...]
        gt_ref[...] = g
    gc = cb_ref[...] + g2 * cw_ref[0:1, :] + g1 * cw_ref[1:2, :] + g * cw_ref[2:3, :]
    act = gc * (1.0 / (1.0 + jnp.exp(-gc))) * u
    act_ref[...] = act.astype(act_ref.dtype)


def ffn_up(x, g, w_il, conv_w, conv_b, *, seq_len=None, prev=None, name):
    m, k = x.shape
    dff = conv_w.shape[1]
    nt = dff // LANES
    tm = _row_tile(m)
    seq_mode = prev is None
    x_spec = pl.BlockSpec((tm, k), lambda i, j: (i, 0))
    common = [pl.BlockSpec((1, k), lambda i, j: (0, 0)),
              pl.BlockSpec((k, 2 * LANES), lambda i, j: (0, j)),
              pl.BlockSpec((CONV_W, LANES), lambda i, j: (0, j)),
              pl.BlockSpec((1, LANES), lambda i, j: (0, j))]
    cargs = [g.reshape(1, k), w_il, conv_w, conv_b.reshape(1, dff)]
    act_spec = pl.BlockSpec((tm, LANES), lambda i, j: (i, j))
    if seq_mode:
        tpb = tm // SUBLANES
        in_specs = [x_spec, pl.BlockSpec((SUBLANES, k), lambda i, j: (jnp.maximum(i * tpb - 1, 0), 0))] + common
        args = [x, x] + cargs
        out_shape = [jax.ShapeDtypeStruct((m, dff), BF16), jax.ShapeDtypeStruct((m // tm, SUBLANES, dff), F32)]
        out_specs = [act_spec, pl.BlockSpec((1, SUBLANES, LANES), lambda i, j: (i, 0, j))]
        scratch = [pltpu.VMEM((tm, k), BF16), pltpu.VMEM((SUBLANES, k), BF16)]
        bps = seq_len // tm
    else:
        pspec = pl.BlockSpec((tm, LANES), lambda i, j: (i, j))
        in_specs = [x_spec, pspec, pspec] + common
        args = [x, prev[0], prev[1]] + cargs
        out_shape = [jax.ShapeDtypeStruct((m, dff), BF16), jax.ShapeDtypeStruct((m, dff), F32)]
        out_specs = [act_spec, pl.BlockSpec((tm, LANES), lambda i, j: (i, j))]
        scratch = [pltpu.VMEM((tm, k), BF16)]
        bps = 1
    body = functools.partial(_ffn_up_body, seq_mode=seq_mode, blocks_per_seq=bps)
    return pl.pallas_call(
        body, grid=(m // tm, nt), in_specs=in_specs, out_specs=out_specs, out_shape=out_shape,
        scratch_shapes=scratch, compiler_params=_cparams(("parallel", "arbitrary")), name=name)(*args)


def _log_sigmoid(x):
    return jnp.minimum(x, 0.0) - jnp.log(1.0 + jnp.exp(-jnp.abs(x)))


def _mlstm_body(proj_ref, gc_ref, gr_ref, bc_ref, br_ref, hg_ref, c0_ref, n0_ref, m0_ref,
                h_ref, c_ref, n_ref, m_ref, *, n_valid, dk, dv):
    c_idx = pl.program_id(1)
    L = proj_ref.shape[1]
    H = A_HEADS

    @pl.when(c_idx == 0)
    def _():
        c_ref[...] = c0_ref[...]
        n_ref[...] = n0_ref[...]
        m_ref[...] = m0_ref[...]

    gates_c = gc_ref[0] + bc_ref[...]
    gates_r = gr_ref[0] + br_ref[...]
    row = lax.broadcasted_iota(jnp.int32, (L, L), 0)
    col = lax.broadcasted_iota(jnp.int32, (L, L), 1)
    tri = col <= row
    if n_valid < L:
        vc = lax.broadcasted_iota(jnp.int32, (L, 1), 0) < n_valid
        vr = lax.broadcasted_iota(jnp.int32, (1, L), 1) < n_valid

    s1, s2, s3 = H * dk, 2 * H * dk, 2 * H * dk + H * dv
    for h in range(H):
        q = proj_ref[0, :, h * dk:(h + 1) * dk] * (dk ** -0.5)
        k = proj_ref[0, :, s1 + h * dk:s1 + (h + 1) * dk]
        v = proj_ref[0, :, s2 + h * dv:s2 + (h + 1) * dv]
        og = proj_ref[0, :, s3 + h * dv:s3 + (h + 1) * dv]
        ig_c = gates_c[:, h:h + 1]
        lf_c = _log_sigmoid(gates_c[:, H + h:H + h + 1])
        ig_r = gates_r[h:h + 1, :]
        lf_r = _log_sigmoid(gates_r[H + h:H + h + 1, :])
        if n_valid < L:
            ig_c, lf_c = jnp.where(vc, ig_c, NEG), jnp.where(vc, lf_c, 0.0)
            ig_r, lf_r = jnp.where(vr, ig_r, NEG), jnp.where(vr, lf_r, 0.0)
        b_c = jnp.sum(jnp.where(tri, lf_r, 0.0), axis=1, keepdims=True)
        b_r = jnp.sum(jnp.where(row <= col, lf_c, 0.0), axis=0, keepdims=True)
        m_prev = m_ref[0, h:h + 1, :]
        logd = jnp.where(tri, b_c - b_r + ig_r, NEG)
        m_t = jnp.maximum(b_c + m_prev, jnp.max(logd, axis=1, keepdims=True))
        d = jnp.exp(logd - m_t)
        inter = jnp.exp(b_c + m_prev - m_t)
        qb, kb, vb = q.astype(BF16), k.astype(BF16), v.astype(BF16)
        s = lax.dot_general(qb, kb, (((1,), (1,)), ((), ())), preferred_element_type=F32)
        w = s * d
        c_prev = c_ref[0, h]
        n_prev = n_ref[0, h:h + 1, :]
        num = inter * jnp.dot(qb, c_prev.astype(BF16), preferred_element_type=F32) \
            + jnp.dot(w.astype(BF16), vb, preferred_element_type=F32)
        den = inter * jnp.sum(q * n_prev, axis=1, keepdims=True) + jnp.sum(w, axis=1, keepdims=True)
        hh = num / jnp.maximum(jnp.abs(den), jnp.exp(-m_t))
        m_new = m_t[L - 1:L, :]
        b_last = b_c[L - 1:L, :]
        w_s = jnp.exp(b_last - b_c + ig_c - m_new)
        decay = jnp.exp(b_last + m_prev - m_new)
        kw = k * w_s
        c_ref[0, h] = decay * c_prev + lax.dot_general(
            kw.astype(BF16), vb, (((0,), (0,)), ((), ())), preferred_element_type=F32)
        n_ref[0, h:h + 1, :] = decay * n_prev + jnp.sum(kw, axis=0, keepdims=True)
        m_ref[0, h:h + 1, :] = m_new
        hn = hh * lax.rsqrt(jnp.mean(hh * hh, axis=1, keepdims=True) + EPS)
        hn = hn * hg_ref[:, h * dv:(h + 1) * dv] * (1.0 / (1.0 + jnp.exp(-og)))
        h_ref[0, :, h * dv:(h + 1) * dv] = hn.astype(h_ref.dtype)


def mlstm(proj, gates, b_if, head_g, c0, n0, m0, *, n_valid, name):
    bsz, t, _ = proj.shape
    H = A_HEADS
    dk, dv = c0.shape[2], c0.shape[3]
    L = min(A_CHUNK, t)
    nc = t // L
    gates_r = jnp.swapaxes(gates[:, :, :2 * H], 1, 2)
    bias_c = jnp.zeros((1, LANES), F32).at[0, :2 * H].set(b_if)
    bias_r = b_if.reshape(2 * H, 1)
    body = functools.partial(_mlstm_body, n_valid=n_valid, dk=dk, dv=dv)
    return pl.pallas_call(
        body, grid=(bsz, nc),
        in_specs=[pl.BlockSpec((1, L, proj.shape[2]), lambda b, c: (b, c, 0)),
                  pl.BlockSpec((1, L, LANES), lambda b, c: (b, c, 0)),
                  pl.BlockSpec((1, 2 * H, L), lambda b, c: (b, 0, c)),
                  pl.BlockSpec((1, LANES), lambda b, c: (0, 0)),
                  pl.BlockSpec((2 * H, 1), lambda b, c: (0, 0)),
                  pl.BlockSpec((1, H * dv), lambda b, c: (0, 0)),
                  pl.BlockSpec((1, H, dk, dv), lambda b, c: (b, 0, 0, 0)),
                  pl.BlockSpec((1, H, dk), lambda b, c: (b, 0, 0)),
                  pl.BlockSpec((1, H, 1), lambda b, c: (b, 0, 0))],
        out_specs=[pl.BlockSpec((1, L, H * dv), lambda b, c: (b, c, 0)),
                   pl.BlockSpec((1, H, dk, dv), lambda b, c: (b, 0, 0, 0)),
                   pl.BlockSpec((1, H, dk), lambda b, c: (b, 0, 0)),
                   pl.BlockSpec((1, H, 1), lambda b, c: (b, 0, 0))],
        out_shape=[jax.ShapeDtypeStruct((bsz, t, H * dv), BF16),
                   jax.ShapeDtypeStruct((bsz, H, dk, dv), F32),
                   jax.ShapeDtypeStruct((bsz, H, dk), F32),
                   jax.ShapeDtypeStruct((bsz, H, 1), F32)],
        compiler_params=_cparams(("parallel", "arbitrary")), name=name,
    )(proj, gates, gates_r, bias_c, bias_r, head_g.reshape(1, H * dv), c0, n0, m0.reshape(bsz, H, 1))


def _moba_seq_body(q_ref, k_ref, v_ref, sl_ref, o_ref, kaug_ref, vb_ref, kmean_ref, *, nb):
    qi = pl.program_id(2)
    blk = MOBA_BLOCK
    hd = q_ref.shape[2]

    @pl.when(qi == 0)
    def _():
        lane = lax.broadcasted_iota(jnp.int32, (blk, hd), 1)
        kmean_ref[...] = jnp.zeros_like(kmean_ref)
        for n in range(nb):
            kf = k_ref[0, n * blk:(n + 1) * blk, :]
            kaug_ref[n * blk:(n + 1) * blk, :hd] = kf.astype(BF16)
            kaug_ref[n * blk:(n + 1) * blk, hd:] = jnp.where(lane == n, 1.0, 0.0).astype(BF16)
            kmean_ref[n:n + 1, :] = jnp.mean(kf, axis=0, keepdims=True)
            vb_ref[n * blk:(n + 1) * blk, :] = v_ref[0, n * blk:(n + 1) * blk, :].astype(BF16)

    q = q_ref[0]
    slope = sl_ref[0]
    slope1 = slope[:, 0:1]
    gate = lax.dot_general(q, kmean_ref[...], (((1,), (1,)), ((), ())),
                           precision=lax.Precision.HIGHEST, preferred_element_type=F32)
    lane = lax.broadcasted_iota(jnp.int32, gate.shape, 1)
    past = lane < qi
    cnt = jnp.zeros(gate.shape, jnp.int32)
    for m in range(nb - 1):
        gm = gate[:, m:m + 1]
        beats = (gm > gate) | ((gm == gate) & (m < lane))
        cnt = cnt + jnp.where(beats, jnp.where(m < qi, 1, 0), 0)
    sel = past & (cnt < MOBA_TOPK)
    selb = jnp.where(sel, 0.0, NEG).astype(BF16)
    qb = q.astype(BF16)
    q_aug = jnp.concatenate([qb, selb], axis=1)

    ql = lax.broadcasted_iota(jnp.int32, (blk, blk), 0)
    kl = lax.broadcasted_iota(jnp.int32, (blk, blk), 1)
    rel = (kl - ql).astype(F32)

    k_own = kaug_ref[pl.ds(pl.multiple_of(qi * blk, blk), blk), :hd]
    v_own = vb_ref[pl.ds(pl.multiple_of(qi * blk, blk), blk), :]
    s = lax.dot_general(qb, k_own, (((1,), (1,)), ((), ())), preferred_element_type=F32)
    s = jnp.where(kl <= ql, s + slope1 * rel, NEG)
    m_run = jnp.max(s, axis=1, keepdims=True)
    p = jnp.exp(s - m_run)
    l_run = jnp.sum(p, axis=1, keepdims=True)
    acc = jnp.dot(p.astype(BF16), v_own, preferred_element_type=F32)

    def body(n, carry):
        m_run, l_run, acc = carry
        off = pl.multiple_of(n * blk, blk)
        kb = kaug_ref[pl.ds(off, blk), :]
        s = lax.dot_general(q_aug, kb, (((1,), (1,)), ((), ())), preferred_element_type=F32)
        s = s + slope1 * (rel + ((n - qi) * blk).astype(F32))
        m_new = jnp.maximum(m_run, jnp.max(s, axis=1, keepdims=True))
        alpha = jnp.exp(m_run - m_new)
        p = jnp.exp(s - m_new)
        l_new = alpha * l_run + jnp.sum(p, axis=1, keepdims=True)
        acc_new = alpha * acc + jnp.dot(p.astype(BF16), vb_ref[pl.ds(off, blk), :], preferred_element_type=F32)
        return m_new, l_new, acc_new

    m_run, l_run, acc = lax.fori_loop(0, qi, body, (m_run, l_run, acc))
    o_ref[0] = (acc / l_run).astype(o_ref.dtype)


def moba_seq(q, kv, slopes, *, name):
    bsz, t, d = q.shape
    H = B_HEADS
    hd = d // H
    blk = MOBA_BLOCK
    nb = t // blk
    assert hd == LANES and nb <= LANES
    body = functools.partial(_moba_seq_body, nb=nb)
    return pl.pallas_call(
        body, grid=(bsz, H, nb),
        in_specs=[pl.BlockSpec((1, blk, hd), lambda b, h, i: (b, i, h)),
                  pl.BlockSpec((1, t, hd), lambda b, h, i: (b, 0, h)),
                  pl.BlockSpec((1, t, hd), lambda b, h, i: (b, 0, H + h)),
                  pl.BlockSpec((1, 1, LANES), lambda b, h, i: (h, 0, 0))],
        out_specs=pl.BlockSpec((1, blk, hd), lambda b, h, i: (b, i, h)),
        out_shape=jax.ShapeDtypeStruct((bsz, t, d), BF16),
        scratch_shapes=[pltpu.VMEM((t, 2 * hd), BF16), pltpu.VMEM((t, hd), BF16), pltpu.VMEM((LANES, hd), F32)],
        compiler_params=_cparams(("parallel", "parallel", "arbitrary")), name=name,
    )(q, kv, kv, jnp.broadcast_to(slopes.reshape(H, 1, 1), (H, 1, LANES)))


def _page_mean_body(pt_ref, k0_ref, k1_ref, o_ref):
    s = jnp.sum(k0_ref[0], axis=0) + jnp.sum(k1_ref[0], axis=0)
    o_ref[0, 0] = s * (1.0 / MOBA_BLOCK)


def paged_block_means(cache_k, page_table):
    _, page, H, hd = cache_k.shape
    bsz, n_pages = page_table.shape
    assert MOBA_BLOCK == 2 * page
    nblk = n_pages // 2
    grid_spec = pltpu.PrefetchScalarGridSpec(
        num_scalar_prefetch=1, grid=(bsz, nblk),
        in_specs=[pl.BlockSpec((1, page, H, hd), lambda b, n, pt: (pt[b, 2 * n], 0, 0, 0)),
                  pl.BlockSpec((1, page, H, hd), lambda b, n, pt: (pt[b, 2 * n + 1], 0, 0, 0))],
        out_specs=pl.BlockSpec((1, 1, H, hd), lambda b, n, pt: (b, n, 0, 0)))
    return pl.pallas_call(
        _page_mean_body, grid_spec=grid_spec,
        out_shape=jax.ShapeDtypeStruct((bsz, nblk, H, hd), F32),
        compiler_params=_cparams(("parallel", "arbitrary")), name="paged_block_means",
    )(page_table, cache_k, cache_k)


def _decode_select_body(q_ref, km_ref, idx_ref, *, nblk):
    q = q_ref[0]
    lane = lax.broadcasted_iota(jnp.int32, (q.shape[0], LANES), 1)
    gate = jnp.full((q.shape[0], LANES), NEG, F32)
    for n in range(nblk):
        gn = jnp.sum(q * km_ref[0, n], axis=1, keepdims=True)
        gate = jnp.where(lane == n, gn, gate)
    out = jnp.zeros((q.shape[0], LANES), jnp.int32)
    for j in range(MOBA_TOPK):
        mx = jnp.max(gate, axis=1, keepdims=True)
        am = jnp.min(jnp.where(gate == mx, lane, LANES), axis=1, keepdims=True)
        out = jnp.where(lane == j, am, out)
        gate = jnp.where(lane == am, -jnp.inf, gate)
    idx_ref[0] = out


def decode_select(q, kmean):
    bsz, H, hd = q.shape
    nblk = kmean.shape[1]
    assert MOBA_TOPK <= nblk <= LANES
    return pl.pallas_call(
        functools.partial(_decode_select_body, nblk=nblk), grid=(bsz,),
        in_specs=[pl.BlockSpec((1, H, hd), lambda b: (b, 0, 0)),
                  pl.BlockSpec((1, nblk, H, hd), lambda b: (b, 0, 0, 0))],
        out_specs=pl.BlockSpec((1, H, LANES), lambda b: (b, 0, 0)),
        out_shape=jax.ShapeDtypeStruct((bsz, H, LANES), jnp.int32),
        compiler_params=_cparams(("parallel",)), name="decode_select")(q, kmean)


def _decode_attend_body(idx_ref, pt_ref, q_ref, ks_ref, vs_ref, sl_ref, k_hbm, v_hbm, o_ref,
                        kbuf, vbuf, sems, *, n_heads, n_pages, page, past_len):
    s = pl.program_id(0)
    n_steps = pl.num_programs(0)
    ppb = MOBA_BLOCK // page
    n_slab = MOBA_TOPK * ppb

    def slab_copies(step, slot):
        b, h = step // n_heads, step % n_heads
        cps = []
        for t in range(n_slab):
            blk_id = idx_ref[step * MOBA_TOPK + t // ppb]
            pg = pt_ref[b * n_pages + blk_id * ppb + t % ppb]
            cps.append(pltpu.make_async_copy(k_hbm.at[pg, :, h, :], kbuf.at[slot, t], sems.at[slot, 0]))
            cps.append(pltpu.make_async_copy(v_hbm.at[pg, :, h, :], vbuf.at[slot, t], sems.at[slot, 1]))
        return cps

    @pl.when(s == 0)
    def _():
        for cp in slab_copies(s, 0):
            cp.start()

    @pl.when(s + 1 < n_steps)
    def _():
        for cp in slab_copies(s + 1, (s + 1) % 2):
            cp.start()

    slot = s % 2
    for cp in slab_copies(s, slot):
        cp.wait()

    q = q_ref[0]
    slope = sl_ref[0][:, 0:1]
    sub = lax.broadcasted_iota(jnp.int32, (page, 1), 0)
    s_self = jnp.sum(q * ks_ref[0], axis=1, keepdims=True)
    scores = []
    for t in range(n_slab):
        blk_id = idx_ref[s * MOBA_TOPK + t // ppb]
        kpos = blk_id * MOBA_BLOCK + (t % ppb) * page + sub
        dist = (past_len - kpos).astype(F32)
        scores.append(jnp.sum(kbuf[slot, t] * q, axis=1, keepdims=True) - slope * dist)
    mx = s_self
    for sc in scores:
        mx = jnp.maximum(mx, jnp.max(sc, axis=0, keepdims=True))
    p_self = jnp.exp(s_self - mx)
    den = p_self
    acc = p_self * vs_ref[0]
    for t, sc in enumerate(scores):
        p = jnp.exp(sc - mx)
        den = den + jnp.sum(p, axis=0, keepdims=True)
        acc = acc + jnp.sum(p * vbuf[slot, t], axis=0, keepdims=True)
    o_ref[0] = acc / den


def decode_attend(q, k_self, v_self, idx, cache_k, cache_v, page_table, slopes, *, past_len):
    n, _, hd = q.shape
    _, page, H, _ = cache_k.shape
    bsz, n_pages = page_table.shape
    n_slab = MOBA_TOPK * (MOBA_BLOCK // page)
    row = pl.BlockSpec((1, 1, hd), lambda s, idx_r, pt_r: (s, 0, 0))
    grid_spec = pltpu.PrefetchScalarGridSpec(
        num_scalar_prefetch=2, grid=(n,),
        in_specs=[row, row, row,
                  pl.BlockSpec((1, 1, LANES), lambda s, idx_r, pt_r: (s % H, 0, 0)),
                  pl.BlockSpec(memory_space=pl.ANY), pl.BlockSpec(memory_space=pl.ANY)],
        out_specs=row,
        scratch_shapes=[pltpu.VMEM((2, n_slab, page, hd), F32), pltpu.VMEM((2, n_slab, page, hd), F32),
                        pltpu.SemaphoreType.DMA((2, 2))])
    body = functools.partial(_decode_attend_body, n_heads=H, n_pages=n_pages, page=page, past_len=past_len)
    return pl.pallas_call(
        body, grid_spec=grid_spec, out_shape=jax.ShapeDtypeStruct((n, 1, hd), F32),
        compiler_params=_cparams(("arbitrary",)), name="decode_attend",
    )(idx, page_table.reshape(-1), q, k_self, v_self,
      jnp.broadcast_to(slopes.reshape(H, 1, 1), (H, 1, LANES)), cache_k, cache_v)


def _interleave_up(w_up, dff):
    k = w_up.shape[0]
    nt = dff // LANES
    return w_up.reshape(k, 2, nt, LANES).transpose(0, 2, 1, 3).reshape(k, 2 * dff).astype(BF16)


def kernel(x_prompt, x_sample, state_mlstm_C, state_mlstm_n, state_mlstm_m, state_conv, cache_k, cache_v, page_table, a_norm_g, a_w_in, a_b_if, a_head_g, a_w_out, kv_norm_g, w_kv, k_norm_g, b_norm_g, b_w_q, q_norm_g, b_w_o, f_norm_g, f_w_up, f_conv_w, f_conv_b, f_w_down):
    bsz, seq, d = x_prompt.shape
    dbsz, dseq, _ = x_sample.shape
    assert dseq == 1
    n_a, depth = a_w_in.shape[0], f_w_up.shape[0]
    n_b = depth - n_a
    dff = f_conv_w.shape[2]
    H = A_HEADS
    dk, dv = state_mlstm_C.shape[3], state_mlstm_C.shape[4]
    hd = d // B_HEADS
    past_len = page_table.shape[1] * cache_k.shape[1]
    s_main = 2 * H * dk + 2 * H * dv
    slopes = jnp.exp2(-8.0 * jnp.arange(1, B_HEADS + 1, dtype=F32) / B_HEADS)

    xp = x_prompt.reshape(bsz * seq, d)
    xs = x_sample.reshape(dbsz, d)
    Cp, Np, Mp, Cs, Ns, Ms, conv_p, conv_s = [], [], [], [], [], [], [], []

    def conv_ffn(layer, xp, xs):
        w_il = _interleave_up(f_w_up[layer], dff)
        w_down = f_w_down[layer].astype(BF16)
        act_p, gt_p = ffn_up(xp, f_norm_g[layer], w_il, f_conv_w[layer], f_conv_b[layer],
                             seq_len=seq, name=f"ffn_up_p{layer}")
        xp = matmul_res(act_p, w_down, xp, name=f"ffn_down_p{layer}")
        conv_p.append(gt_p.reshape(bsz, -1, SUBLANES, dff)[:, -1, SUBLANES - (CONV_W - 1):, :])
        buf = state_conv[layer]
        act_s, g_s = ffn_up(xs, f_norm_g[layer], w_il, f_conv_w[layer], f_conv_b[layer],
                            prev=(buf[:, 0, :], buf[:, 1, :]), name=f"ffn_up_s{layer}")
        xs = matmul_res(act_s, w_down, xs, name=f"ffn_down_s{layer}")
        conv_s.append(jnp.stack([buf[:, 1, :], g_s], axis=1))
        return xp, xs

    for a in range(n_a):
        w_main = a_w_in[a][:, :s_main].astype(BF16)
        w_gate = jnp.zeros((d, LANES), BF16).at[:, :2 * H].set(a_w_in[a][:, s_main:].astype(BF16))
        w_out = a_w_out[a].astype(BF16)
        proj_p, gates_p = norm_matmul(xp, a_norm_g[a], w_main, w_side=w_gate, name=f"mlstm_in_p{a}")
        h_p, c1, n1, m1 = mlstm(
            proj_p.reshape(bsz, seq, s_main), gates_p.reshape(bsz, seq, LANES), a_b_if[a], a_head_g[a],
            jnp.zeros((bsz, H, dk, dv), F32), jnp.zeros((bsz, H, dk), F32), jnp.zeros((bsz, H), F32),
            n_valid=A_CHUNK, name=f"mlstm_p{a}")
        xp = matmul_res(h_p.reshape(bsz * seq, H * dv), w_out, xp, name=f"mlstm_out_p{a}")
        proj_s, gates_s = norm_matmul(xs, a_norm_g[a], w_main, w_side=w_gate, name=f"mlstm_in_s{a}")
        proj_s = jnp.zeros((dbsz, A_CHUNK, s_main), F32).at[:, 0, :].set(proj_s)
        gates_s = jnp.zeros((dbsz, A_CHUNK, LANES), F32).at[:, 0, :].set(gates_s)
        h_s, c2, n2, m2 = mlstm(proj_s, gates_s, a_b_if[a], a_head_g[a],
                                state_mlstm_C[a], state_mlstm_n[a], state_mlstm_m[a],
                                n_valid=1, name=f"mlstm_s{a}")
        xs = matmul_res(h_s[:, 0, :], w_out, xs, name=f"mlstm_out_s{a}")
        Cp.append(c1), Np.append(n1), Mp.append(m1.reshape(bsz, H))
        Cs.append(c2), Ns.append(n2), Ms.append(m2.reshape(dbsz, H))
        xp, xs = conv_ffn(a, xp, xs)

    w_kv_b = w_kv.astype(BF16)
    kv_p = norm_matmul(xp, kv_norm_g, w_kv_b, head_g=k_norm_g, n_norm_cols=d, name="shared_kv_p")
    kv_s = norm_matmul(xs, kv_norm_g, w_kv_b, head_g=k_norm_g, n_norm_cols=d, name="shared_kv_s")
    k_s, v_s = kv_s[:, :d], kv_s[:, d:]
    kmean_s = paged_block_means(cache_k, page_table)
    kv_p3 = kv_p.reshape(bsz, seq, 2 * d)

    for bl in range(n_b):
        w_q = b_w_q[bl].astype(BF16)
        w_o = b_w_o[bl].astype(BF16)
        q_p = norm_matmul(xp, b_norm_g[bl], w_q, head_g=q_norm_g[bl], n_norm_cols=d,
                          head_scale=hd ** -0.5, name=f"moba_q_p{bl}")
        o_p = moba_seq(q_p.reshape(bsz, seq, d), kv_p3, slopes, name=f"moba_attn_p{bl}")
        xp = matmul_res(o_p.reshape(bsz * seq, d), w_o, xp, name=f"moba_out_p{bl}")
        q_s = norm_matmul(xs, b_norm_g[bl], w_q, head_g=q_norm_g[bl], n_norm_cols=d,
                          head_scale=hd ** -0.5, name=f"moba_q_s{bl}")
        idx = decode_select(q_s.reshape(dbsz, B_HEADS, hd), kmean_s)[:, :, :MOBA_TOPK].reshape(-1)
        o_s = decode_attend(q_s.reshape(dbsz * B_HEADS, 1, hd), k_s.reshape(dbsz * B_HEADS, 1, hd),
                            v_s.reshape(dbsz * B_HEADS, 1, hd), idx, cache_k, cache_v, page_table,
                            slopes, past_len=past_len)
        xs = matmul_res(o_s.reshape(dbsz, d).astype(BF16), w_o, xs, name=f"moba_out_s{bl}")
        xp, xs = conv_ffn(n_a + bl, xp, xs)

    return (xp.reshape(bsz, seq, d), xs.reshape(dbsz, 1, d),
            jnp.stack(Cp), jnp.stack(Np), jnp.stack(Mp),
            jnp.stack(Cs), jnp.stack(Ns), jnp.stack(Ms),
            jnp.stack(conv_p), jnp.stack(conv_s),
            kv_p3[:, :, :d].reshape(bsz, seq, B_HEADS, hd), kv_p3[:, :, d:].reshape(bsz, seq, B_HEADS, hd),
            k_s.reshape(dbsz, 1, B_HEADS, hd), v_s.reshape(dbsz, 1, B_HEADS, hd))
```

```python
import functools
import math

import jax
import jax.numpy as jnp
from jax import lax
from jax.experimental import pallas as pl
from jax.experimental.pallas import tpu as pltpu

F32 = jnp.float32
BF16 = jnp.bfloat16

EPS = 1e-6
NEG = -1e30
LANES = 128
SUBLANES = 8
MXU_N = 256
VMEM_LIMIT = 56 * 1024 * 1024

A_HEADS = 8
A_CHUNK = 128
B_HEADS = 16
MOBA_BLOCK = 256
MOBA_TOPK = 3
MOBA_SEL_ROWS = 16
MOBA_HEADS_PER_STEP = 2
CONV_W = 3
FFN_TILE = 512


def _cparams(sem):
    return pltpu.CompilerParams(dimension_semantics=sem, vmem_limit_bytes=VMEM_LIMIT)


def _row_tile(m):
    return min(m, 1024)


def _col_tile(n, cap=1024):
    for t in (1024, 512, 256, 128):
        if t <= cap and n % t == 0:
            return t
    return n


def _split2(a):
    hi = a.astype(BF16)
    return hi, (a - hi.astype(F32)).astype(BF16)


def _mm(a, w):
    if w.dtype == BF16:
        return jnp.dot(a.astype(BF16), w, preferred_element_type=F32)
    (a_hi, a_lo), (w_hi, w_lo) = _split2(a), _split2(w)
    return (jnp.dot(a_hi, w_hi, preferred_element_type=F32) + jnp.dot(a_lo, w_hi, preferred_element_type=F32)
            + jnp.dot(a_hi, w_lo, preferred_element_type=F32))


def _bf16_round(x):
    return x.astype(BF16).astype(F32)


def _norm_matmul_body(*refs, n_norm_tiles, n_tiles, head_scale, has_side, heads_out):
    it = iter(refs)
    x_ref, g_ref, w_ref = next(it), next(it), next(it)
    hg_ref = next(it) if n_norm_tiles else None
    w2_ref = next(it) if has_side else None
    o_ref = next(it)
    o2_ref = next(it) if has_side else None
    ha_ref, hb_ref = (next(it), next(it)) if heads_out else (None, None)
    xn_ref = next(it)
    j = pl.program_id(1)

    @pl.when(j == 0)
    def _():
        x = x_ref[...]
        ms = jnp.mean(x * x, axis=-1, keepdims=True)
        xn = (x * lax.rsqrt(ms + EPS) * g_ref[...]).astype(xn_ref.dtype)
        xn_ref[...] = xn
        if has_side:
            o2_ref[...] = _mm(xn, w2_ref[...])

    acc = _mm(xn_ref[...], w_ref[...])

    def headnorm(a):
        parts = []
        for c in range(a.shape[1] // LANES):
            p = a[:, c * LANES:(c + 1) * LANES]
            ms = jnp.mean(p * p, axis=-1, keepdims=True)
            p = p * lax.rsqrt(ms + EPS) * hg_ref[...]
            if head_scale != 1.0:
                p = p * head_scale
            parts.append(p)
        return jnp.concatenate(parts, axis=1) if len(parts) > 1 else parts[0]

    def store(r, h_ref):
        o_ref[...] = r.astype(o_ref.dtype)
        if h_ref is not None:
            for h in range(r.shape[1] // LANES):
                h_ref[:, h, :] = r[:, h * LANES:(h + 1) * LANES]

    if n_norm_tiles == 0:
        store(acc, None)
    elif n_norm_tiles >= n_tiles:
        store(headnorm(acc), None)
    else:
        @pl.when(j < n_norm_tiles)
        def _():
            store(headnorm(acc), ha_ref)

        @pl.when(j >= n_norm_tiles)
        def _():
            store(acc, hb_ref)


def _weight_spec(w, layer, k, tn, col0=0):
    if w.ndim == 2:
        return pl.BlockSpec((k, tn), lambda i, j: (0, col0 + j))
    return pl.BlockSpec((None, k, tn), lambda i, j: (layer, 0, col0 + j))


def norm_matmul(x, g, w, *, w_layer=None, n_cols=None, head_g=None, n_norm_cols=0, head_scale=1.0,
                w_side=None, heads_out=False, name):
    m, k = x.shape
    n = w.shape[-1] if n_cols is None else n_cols
    tm, tn = _row_tile(m), _col_tile(n, cap=1024 if w.dtype == BF16 else 512)
    if heads_out:
        tm = min(tm, 512)
    n_tiles = n // tn
    assert n_norm_cols % tn == 0
    n_norm_tiles = n_norm_cols // tn
    assert not heads_out or (2 * n_norm_tiles == n_tiles and w_side is None and tn // LANES >= SUBLANES)
    in_specs = [pl.BlockSpec((tm, k), lambda i, j: (i, 0)),
                pl.BlockSpec((1, k), lambda i, j: (0, 0)),
                _weight_spec(w, w_layer, k, tn)]
    args = [x, g.reshape(1, k), w]
    if n_norm_tiles:
        in_specs.append(pl.BlockSpec((1, LANES), lambda i, j: (0, 0)))
        args.append(head_g.reshape(1, LANES))
    out_shape = [jax.ShapeDtypeStruct((m, n), F32)]
    out_specs = [pl.BlockSpec((tm, tn), lambda i, j: (i, j))]
    if w_side is not None:
        in_specs.append(pl.BlockSpec((k, LANES), lambda i, j: (0, 0)))
        args.append(w_side)
        out_shape.append(jax.ShapeDtypeStruct((m, LANES), F32))
        out_specs.append(pl.BlockSpec((tm, LANES), lambda i, j: (i, 0)))
    if heads_out:
        hpt = tn // LANES
        half = n_tiles // 2
        hshape = jax.ShapeDtypeStruct((m, half * hpt, LANES), F32)
        out_shape += [hshape, hshape]
        out_specs += [pl.BlockSpec((tm, hpt, LANES), lambda i, j: (i, jnp.minimum(j, half - 1), 0)),
                      pl.BlockSpec((tm, hpt, LANES), lambda i, j: (i, jnp.maximum(j - half, 0), 0))]
    body = functools.partial(_norm_matmul_body, n_norm_tiles=n_norm_tiles, n_tiles=n_tiles,
                             head_scale=head_scale, has_side=w_side is not None, heads_out=heads_out)
    outs = pl.pallas_call(
        body, grid=(m // tm, n_tiles), in_specs=in_specs, out_specs=out_specs, out_shape=out_shape,
        scratch_shapes=[pltpu.VMEM((tm, k), w.dtype)],
        compiler_params=_cparams(("parallel", "arbitrary")), name=name)(*args)
    return outs if len(outs) > 1 else outs[0]


def _matmul_res_body(a_ref, w_ref, r_ref, o_ref):
    o_ref[...] = r_ref[...] + _mm(a_ref[...], w_ref[...])


def matmul_res(a, w, res, *, w_layer=None, name):
    m, k = a.shape
    n = w.shape[-1]
    tm, tn = _row_tile(m), _col_tile(n, cap=512 if w.dtype == BF16 else 256)
    return pl.pallas_call(
        _matmul_res_body, grid=(m // tm, n // tn),
        in_specs=[pl.BlockSpec((tm, k), lambda i, j: (i, 0)),
                  _weight_spec(w, w_layer, k, tn),
                  pl.BlockSpec((tm, tn), lambda i, j: (i, j))],
        out_specs=pl.BlockSpec((tm, tn), lambda i, j: (i, j)),
        out_shape=jax.ShapeDtypeStruct((m, n), F32),
        compiler_params=_cparams(("parallel", "arbitrary")), name=name)(a, w, res)


def _ffn_up_body(*refs, seq_mode, blocks_per_seq):
    if seq_mode:
        x_ref, xh_ref, g_ref, wu_ref, wg_ref, cw_ref, cb_ref, act_ref, gt_ref, xn_ref, xhn_ref = refs
    else:
        x_ref, p2_ref, p1_ref, g_ref, wu_ref, wg_ref, cw_ref, cb_ref, act_ref, gt_ref, xn_ref = refs
    i, j = pl.program_id(0), pl.program_id(1)

    def norm(x):
        ms = jnp.mean(x * x, axis=-1, keepdims=True)
        return (x * lax.rsqrt(ms + EPS) * g_ref[...]).astype(xn_ref.dtype)

    @pl.when(j == 0)
    def _():
        xn_ref[...] = norm(x_ref[...])
        if seq_mode:
            xhn_ref[...] = norm(xh_ref[...])

    xn = xn_ref[...]
    tm = xn.shape[0]
    tw = wu_ref.shape[1]
    cw_ = min(tw, MXU_N)
    for c in range(tw // cw_):
        cs = slice(c * cw_, (c + 1) * cw_)
        u = _mm(xn, wu_ref[:, cs])
        g = _mm(xn, wg_ref[:, cs])
        if seq_mode:
            gh = _mm(xhn_ref[...], wg_ref[:, cs])
            gh = jnp.where(i % blocks_per_seq == 0, 0.0, gh)
            prev1, prev2 = gh[SUBLANES - 1:SUBLANES, :], gh[SUBLANES - 2:SUBLANES - 1, :]
            row = lax.broadcasted_iota(jnp.int32, (SUBLANES, cw_), 0)
            r1, r2 = pltpu.roll(g, 1, 0), pltpu.roll(g, 2, 0)
            g1 = jnp.concatenate([jnp.where(row == 0, prev1, r1[:SUBLANES])] + ([r1[SUBLANES:]] if tm > SUBLANES else []), axis=0)
            g2 = jnp.concatenate([jnp.where(row == 0, prev2, jnp.where(row == 1, prev1, r2[:SUBLANES]))]
                                 + ([r2[SUBLANES:]] if tm > SUBLANES else []), axis=0)
            gt_ref[0, :, cs] = g[tm - SUBLANES:, :]
        else:
            g1, g2 = p1_ref[:, cs], p2_ref[:, cs]
            gt_ref[:, cs] = g
        gc = cb_ref[:, cs] + g2 * cw_ref[0:1, cs] + g1 * cw_ref[1:2, cs] + g * cw_ref[2:3, cs]
        act = gc * (1.0 / (1.0 + jnp.exp(-gc))) * u
        act_ref[:, cs] = act.astype(act_ref.dtype)


def ffn_up(x, g, w_up, conv_w, conv_b, *, tile, w_layer=None, seq_len=None, prev=None, name):
    m, k = x.shape
    dff = conv_w.shape[1]
    tw = tile
    assert dff % tw == 0
    nt = dff // tw
    wdt = w_up.dtype
    tm = _row_tile(m)
    seq_mode = prev is None
    x_spec = pl.BlockSpec((tm, k), lambda i, j: (i, 0))
    common = [pl.BlockSpec((1, k), lambda i, j: (0, 0)),
              _weight_spec(w_up, w_layer, k, tw),
              _weight_spec(w_up, w_layer, k, tw, col0=nt),
              pl.BlockSpec((CONV_W, tw), lambda i, j: (0, j)),
              pl.BlockSpec((1, tw), lambda i, j: (0, j))]
    cargs = [g.reshape(1, k), w_up, w_up, conv_w, conv_b.reshape(1, dff)]
    act_spec = pl.BlockSpec((tm, tw), lambda i, j: (i, j))
    if seq_mode:
        tpb = tm // SUBLANES
        in_specs = [x_spec, pl.BlockSpec((SUBLANES, k), lambda i, j: (jnp.maximum(i * tpb - 1, 0), 0))] + common
        args = [x, x] + cargs
        out_shape = [jax.ShapeDtypeStruct((m, dff), wdt), jax.ShapeDtypeStruct((m // tm, SUBLANES, dff), F32)]
        out_specs = [act_spec, pl.BlockSpec((1, SUBLANES, tw), lambda i, j: (i, 0, j))]
        scratch = [pltpu.VMEM((tm, k), wdt), pltpu.VMEM((SUBLANES, k), wdt)]
        bps = seq_len // tm
    else:
        pspec = pl.BlockSpec((tm, tw), lambda i, j: (i, j))
        in_specs = [x_spec, pspec, pspec] + common
        args = [x, prev[0], prev[1]] + cargs
        out_shape = [jax.ShapeDtypeStruct((m, dff), wdt), jax.ShapeDtypeStruct((m, dff), F32)]
        out_specs = [act_spec, pl.BlockSpec((tm, tw), lambda i, j: (i, j))]
        scratch = [pltpu.VMEM((tm, k), wdt)]
        bps = 1
    body = functools.partial(_ffn_up_body, seq_mode=seq_mode, blocks_per_seq=bps)
    return pl.pallas_call(
        body, grid=(m // tm, nt), in_specs=in_specs, out_specs=out_specs, out_shape=out_shape,
        scratch_shapes=scratch, compiler_params=_cparams(("parallel", "arbitrary")), name=name)(*args)


def _log_sigmoid(x):
    return jnp.minimum(x, 0.0) - jnp.log(1.0 + jnp.exp(-jnp.abs(x)))


def _mlstm_body(proj_ref, gc_ref, gr_ref, bc_ref, br_ref, hg_ref, c0_ref, n0_ref, m0_ref,
                h_ref, c_ref, n_ref, m_ref, *, n_valid, dk, dv, precise):
    c_idx = pl.program_id(1)
    L = proj_ref.shape[1]
    H = A_HEADS

    @pl.when(c_idx == 0)
    def _():
        c_ref[...] = c0_ref[...]
        n_ref[...] = n0_ref[...]
        m_ref[...] = m0_ref[...]

    gates_c = gc_ref[0] + bc_ref[...]
    gates_r = gr_ref[0] + br_ref[...]
    row = lax.broadcasted_iota(jnp.int32, (L, L), 0)
    col = lax.broadcasted_iota(jnp.int32, (L, L), 1)
    tri = col <= row
    if n_valid < L:
        vc = lax.broadcasted_iota(jnp.int32, (L, 1), 0) < n_valid
        vr = lax.broadcasted_iota(jnp.int32, (1, L), 1) < n_valid

    s1, s2, s3 = H * dk, 2 * H * dk, 2 * H * dk + H * dv
    heads = range(H)

    def gate_stage(h):
        ig_c = gates_c[:, h:h + 1]
        lf_c = _log_sigmoid(gates_c[:, H + h:H + h + 1])
        ig_r = gates_r[h:h + 1, :]
        lf_r = _log_sigmoid(gates_r[H + h:H + h + 1, :])
        if n_valid < L:
            ig_c, lf_c = jnp.where(vc, ig_c, NEG), jnp.where(vc, lf_c, 0.0)
            ig_r, lf_r = jnp.where(vr, ig_r, NEG), jnp.where(vr, lf_r, 0.0)
        b_c = jnp.sum(jnp.where(tri, lf_r, 0.0), axis=1, keepdims=True)
        b_r = jnp.sum(jnp.where(row <= col, lf_c, 0.0), axis=0, keepdims=True)
        m_prev = m_ref[0, h:h + 1, :]
        logd = jnp.where(tri, b_c - b_r + ig_r, NEG)
        m_t = jnp.maximum(b_c + m_prev, jnp.max(logd, axis=1, keepdims=True))
        d = jnp.exp(logd - m_t)
        inter = jnp.exp(b_c + m_prev - m_t)
        m_new = m_t[L - 1:L, :]
        b_last = b_c[L - 1:L, :]
        w_s = jnp.exp(b_last - b_c + ig_c - m_new)
        decay = jnp.exp(b_last + m_prev - m_new)
        return dict(m_t=m_t, d=d, inter=inter, m_new=m_new, w_s=w_s, decay=decay)

    nn, nt, tn = (((1,), (0,)), ((), ())), (((1,), (1,)), ((), ())), (((0,), (0,)), ((), ()))

    def mm(a, b, dims):
        if not precise:
            return lax.dot_general(a.astype(BF16), b.astype(BF16), dims, preferred_element_type=F32)
        (a_hi, a_lo), (b_hi, b_lo) = _split2(a), _split2(b)
        return (lax.dot_general(a_hi, b_hi, dims, preferred_element_type=F32)
                + lax.dot_general(a_lo, b_hi, dims, preferred_element_type=F32)
                + lax.dot_general(a_hi, b_lo, dims, preferred_element_type=F32))

    rnd = (lambda x: x) if precise else _bf16_round

    def state_dots(h, gt):
        q = proj_ref[0, :, h * dk:(h + 1) * dk] * (dk ** -0.5)
        k = proj_ref[0, :, s1 + h * dk:s1 + (h + 1) * dk]
        v = proj_ref[0, :, s2 + h * dv:s2 + (h + 1) * dv]
        v = v if precise else v.astype(BF16)
        kw = k * gt["w_s"]
        return dict(qn=rnd(q), kn=rnd(k) * rnd(gt["w_s"]), v=v,
                    s=mm(q, k, nt), qc=mm(q, c_ref[0, h], nn), kv=mm(kw, v, tn))

    def value_dot(gt, st):
        w = st["s"] * gt["d"]
        return w, mm(w, st["v"], nn)

    def finish(h, gt, st, w, pv):
        n_prev = n_ref[0, h:h + 1, :]
        num = gt["inter"] * st["qc"] + pv
        den = gt["inter"] * jnp.sum(st["qn"] * rnd(n_prev), axis=1, keepdims=True) \
            + jnp.sum(w, axis=1, keepdims=True)
        hh = num / jnp.maximum(jnp.abs(den), jnp.exp(-gt["m_t"]))
        c_ref[0, h] = gt["decay"] * c_ref[0, h] + st["kv"]
        n_ref[0, h:h + 1, :] = gt["decay"] * n_prev + jnp.sum(st["kn"], axis=0, keepdims=True)
        m_ref[0, h:h + 1, :] = gt["m_new"]
        og = proj_ref[0, :, s3 + h * dv:s3 + (h + 1) * dv]
        hn = hh * lax.rsqrt(jnp.mean(hh * hh, axis=1, keepdims=True) + EPS)
        hn = hn * hg_ref[:, h * dv:(h + 1) * dv] * (1.0 / (1.0 + jnp.exp(-og)))
        h_ref[0, :, h * dv:(h + 1) * dv] = hn.astype(h_ref.dtype)

    gts = [gate_stage(h) for h in heads]
    sts = [state_dots(h, gts[h]) for h in heads]
    wps = [value_dot(gts[h], sts[h]) for h in heads]
    for h in heads:
        finish(h, gts[h], sts[h], *wps[h])


def mlstm(proj, gates, b_if, head_g, c0, n0, m0, *, n_valid, precise=False, name):
    bsz, t, _ = proj.shape
    H = A_HEADS
    dk, dv = c0.shape[2], c0.shape[3]
    L = min(A_CHUNK, t)
    nc = t // L
    gates_r = jnp.swapaxes(gates[:, :, :2 * H], 1, 2)
    bias_c = jnp.zeros((1, LANES), F32).at[0, :2 * H].set(b_if)
    bias_r = b_if.reshape(2 * H, 1)
    body = functools.partial(_mlstm_body, n_valid=n_valid, dk=dk, dv=dv, precise=precise)
    return pl.pallas_call(
        body, grid=(bsz, nc),
        in_specs=[pl.BlockSpec((1, L, proj.shape[2]), lambda b, c: (b, c, 0)),
                  pl.BlockSpec((1, L, LANES), lambda b, c: (b, c, 0)),
                  pl.BlockSpec((1, 2 * H, L), lambda b, c: (b, 0, c)),
                  pl.BlockSpec((1, LANES), lambda b, c: (0, 0)),
                  pl.BlockSpec((2 * H, 1), lambda b, c: (0, 0)),
                  pl.BlockSpec((1, H * dv), lambda b, c: (0, 0)),
                  pl.BlockSpec((1, H, dk, dv), lambda b, c: (b, 0, 0, 0)),
                  pl.BlockSpec((1, H, dk), lambda b, c: (b, 0, 0)),
                  pl.BlockSpec((1, H, 1), lambda b, c: (b, 0, 0))],
        out_specs=[pl.BlockSpec((1, L, H * dv), lambda b, c: (b, c, 0)),
                   pl.BlockSpec((1, H, dk, dv), lambda b, c: (b, 0, 0, 0)),
                   pl.BlockSpec((1, H, dk), lambda b, c: (b, 0, 0)),
                   pl.BlockSpec((1, H, 1), lambda b, c: (b, 0, 0))],
        out_shape=[jax.ShapeDtypeStruct((bsz, t, H * dv), F32 if precise else BF16),
                   jax.ShapeDtypeStruct((bsz, H, dk, dv), F32),
                   jax.ShapeDtypeStruct((bsz, H, dk), F32),
                   jax.ShapeDtypeStruct((bsz, H, 1), F32)],
        compiler_params=_cparams(("parallel", "arbitrary")), name=name,
    )(proj, gates, gates_r, bias_c, bias_r, head_g.reshape(1, H * dv), c0, n0, m0.reshape(bsz, H, 1))


def _moba_seq_body(q_ref, k_ref, v_ref, sl_ref, o_ref, kaug_ref, vt_ref, km_ref, mask_ref, *, nb, G):
    qi = pl.program_id(2)
    blk = MOBA_BLOCK
    hd = LANES

    @pl.when(qi == 0)
    def _():
        kk = lax.broadcasted_iota(jnp.int32, (blk, blk), 0)
        qq = lax.broadcasted_iota(jnp.int32, (blk, blk), 1)
        causal = jnp.where(kk <= qq, 0.0, NEG)
        mask_ref[0, :blk, :] = causal
        mask_ref[0, blk:, :] = jnp.full((blk, blk), NEG, F32)
        mask_ref[1, :blk, :] = jnp.zeros((blk, blk), F32)
        mask_ref[1, blk:, :] = causal
        lane = lax.broadcasted_iota(jnp.int32, (blk, hd), 1)
        row = lax.broadcasted_iota(jnp.int32, (blk, hd), 0)
        ones_row = jnp.where(lax.broadcasted_iota(jnp.int32, (MOBA_SEL_ROWS, blk), 0) == 0, 1.0, 0.0)
        km_ref[...] = jnp.zeros_like(km_ref)
        for g in range(G):
            slope2 = sl_ref[0, g:g + 1, :]
            for n in range(nb):
                j, r0 = n // 2, (n % 2) * blk
                kf = k_ref[0, n * blk:(n + 1) * blk, g * hd:(g + 1) * hd]
                a = (row + n * blk).astype(F32) * slope2
                a_hi = a.astype(BF16)
                a_r = a - a_hi.astype(F32)
                a_mid = a_r.astype(BF16)
                a_lo = (a_r - a_mid.astype(F32)).astype(BF16)
                aug = jnp.where(lane == n, 1.0, 0.0).astype(BF16)
                aug = jnp.where(lane == MOBA_SEL_ROWS, a_hi, aug)
                aug = jnp.where(lane == MOBA_SEL_ROWS + 1, a_mid, aug)
                aug = jnp.where(lane == MOBA_SEL_ROWS + 2, a_lo, aug)
                kaug_ref[g, j, r0:r0 + blk, :hd] = kf.astype(BF16)
                kaug_ref[g, j, r0:r0 + blk, hd:] = aug
                km_ref[g, n:n + 1, :] = jnp.mean(kf, axis=0, keepdims=True)
                vt = v_ref[0, n * blk:(n + 1) * blk, g * hd:(g + 1) * hd].T
                vt_ref[g, j, :hd, r0:r0 + blk] = vt.astype(BF16)
                vt_ref[g, j, hd:, r0:r0 + blk] = ones_row.astype(BF16)

    rowb = lax.broadcasted_iota(jnp.int32, (MOBA_SEL_ROWS, blk), 0)
    ones_rows = jnp.where(rowb < 3, 1.0, 0.0).astype(BF16)
    qaugs = []
    for g in range(G):
        qt = q_ref[0, :, g * hd:(g + 1) * hd].T
        q_hi = qt.astype(BF16)
        gate = jnp.dot(km_ref[g].astype(BF16), q_hi, preferred_element_type=F32)
        cnt = jnp.zeros(gate.shape, jnp.int32)
        for m in range(nb - 1):
            gm = gate[m:m + 1, :]
            beats = (gm > gate) | ((gm == gate) & (m < rowb))
            cnt = cnt + jnp.where(beats, jnp.where(m < qi, 1, 0), 0)
        keep = (rowb >= qi) | (cnt < MOBA_TOPK)
        selb = jnp.where(keep, 0.0, NEG).astype(BF16)
        qaugs.append(jnp.concatenate(
            [q_hi, selb, ones_rows, jnp.zeros((hd - 2 * MOBA_SEL_ROWS, blk), BF16)], axis=0))

    def pair_step(j, carry, mask):
        ss = [jnp.dot(kaug_ref[g, j], qaugs[g], preferred_element_type=F32) for g in range(G)]
        if mask is not None:
            ss = [s + mask for s in ss]
        ms = [jnp.maximum(carry[g][0], jnp.max(ss[g], axis=0, keepdims=True)) for g in range(G)]
        ps = [jnp.exp(ss[g] - ms[g]).astype(BF16) for g in range(G)]
        pvs = [jnp.dot(vt_ref[g, j], ps[g], preferred_element_type=F32) for g in range(G)]
        return tuple((ms[g], jnp.exp(carry[g][0] - ms[g]) * carry[g][1] + pvs[g]) for g in range(G))

    init = tuple((jnp.full((1, blk), -jnp.inf, F32), jnp.zeros((hd + MOBA_SEL_ROWS, blk), F32))
                 for _ in range(G))
    last = (qi + 2) // 2 - 1
    carry = lax.fori_loop(0, last, lambda j, c: pair_step(j, c, None), init)
    carry = pair_step(last, carry, mask_ref[qi % 2])
    for g in range(G):
        _, acc = carry[g]
        o = (acc[:hd] / acc[hd:hd + 1]).T
        o_ref[0, :, g * hd:(g + 1) * hd] = o.astype(o_ref.dtype)


def moba_seq(q, kv, slopes, *, name):
    bsz, t, d = q.shape
    hd = LANES
    H = d // hd
    G = MOBA_HEADS_PER_STEP
    blk = MOBA_BLOCK
    nb = t // blk
    assert H % G == 0 and nb % 2 == 0 and nb <= MOBA_SEL_ROWS
    body = functools.partial(_moba_seq_body, nb=nb, G=G)
    return pl.pallas_call(
        body, grid=(bsz, H // G, nb),
        in_specs=[pl.BlockSpec((1, blk, G * hd), lambda b, h, i: (b, i, h)),
                  pl.BlockSpec((1, t, G * hd), lambda b, h, i: (b, 0, h)),
                  pl.BlockSpec((1, t, G * hd), lambda b, h, i: (b, 0, H // G + h)),
                  pl.BlockSpec((1, G, LANES), lambda b, h, i: (h, 0, 0))],
        out_specs=pl.BlockSpec((1, blk, G * hd), lambda b, h, i: (b, i, h)),
        out_shape=jax.ShapeDtypeStruct((bsz, t, d), BF16),
        scratch_shapes=[pltpu.VMEM((G, nb // 2, 2 * blk, 2 * hd), BF16),
                        pltpu.VMEM((G, nb // 2, hd + MOBA_SEL_ROWS, 2 * blk), BF16),
                        pltpu.VMEM((G, MOBA_SEL_ROWS, hd), F32),
                        pltpu.VMEM((2, 2 * blk, blk), F32)],
        compiler_params=_cparams(("parallel", "parallel", "arbitrary")), name=name,
    )(q, kv, kv, jnp.broadcast_to(slopes.reshape(H // G, G, 1), (H // G, G, LANES)))


def _page_mean_body(pt_ref, *refs, ppb):
    k_refs, o_ref = refs[:-1], refs[-1]
    for t in range(len(k_refs) // ppb):
        s = jnp.sum(k_refs[t * ppb][0], axis=0)
        for r in k_refs[t * ppb + 1:(t + 1) * ppb]:
            s = s + jnp.sum(r[0], axis=0)
        o_ref[0, t] = s * (1.0 / MOBA_BLOCK)


def paged_block_means(cache_k, page_table):
    _, page, H, hd = cache_k.shape
    bsz, n_pages = page_table.shape
    ppb = MOBA_BLOCK // page
    nblk = n_pages // ppb
    bps = 2 if nblk % 2 == 0 else 1
    pps = bps * ppb

    def page_spec(t):
        return pl.BlockSpec((1, page, H, hd), lambda b, n, pt: (pt[b, pps * n + t], 0, 0, 0))

    grid_spec = pltpu.PrefetchScalarGridSpec(
        num_scalar_prefetch=1, grid=(bsz, nblk // bps),
        in_specs=[page_spec(t) for t in range(pps)],
        out_specs=pl.BlockSpec((1, bps, H, hd), lambda b, n, pt: (b, n, 0, 0)))
    return pl.pallas_call(
        functools.partial(_page_mean_body, ppb=ppb), grid_spec=grid_spec,
        out_shape=jax.ShapeDtypeStruct((bsz, nblk, H, hd), F32),
        compiler_params=_cparams(("parallel", "arbitrary")), name="paged_block_means",
    )(page_table, *([cache_k] * pps))


def _decode_select_body(q_ref, km_ref, idx_ref, *, nblk):
    q = q_ref[0]
    lane = lax.broadcasted_iota(jnp.int32, (q.shape[0], LANES), 1)
    gate = jnp.full((q.shape[0], LANES), NEG, F32)
    for n in range(nblk):
        gn = jnp.sum(q * km_ref[0, n], axis=1, keepdims=True)
        gate = jnp.where(lane == n, gn, gate)
    out = jnp.zeros((q.shape[0], LANES), jnp.int32)
    for j in range(MOBA_TOPK):
        mx = jnp.max(gate, axis=1, keepdims=True)
        am = jnp.min(jnp.where(gate == mx, lane, LANES), axis=1, keepdims=True)
        out = jnp.where(lane == j, am, out)
        gate = jnp.where(lane == am, -jnp.inf, gate)
    idx_ref[0] = out


def decode_select(q, kmean):
    bsz, H, hd = q.shape
    nblk = kmean.shape[1]
    assert MOBA_TOPK <= nblk <= LANES
    return pl.pallas_call(
        functools.partial(_decode_select_body, nblk=nblk), grid=(bsz,),
        in_specs=[pl.BlockSpec((1, H, hd), lambda b: (b, 0, 0)),
                  pl.BlockSpec((1, nblk, H, hd), lambda b: (b, 0, 0, 0))],
        out_specs=pl.BlockSpec((1, H, LANES), lambda b: (b, 0, 0)),
        out_shape=jax.ShapeDtypeStruct((bsz, H, LANES), jnp.int32),
        compiler_params=_cparams(("parallel",)), name="decode_select")(q, kmean)


def _decode_attend_body(idx_ref, pt_ref, q_ref, ks_ref, vs_ref, sl_ref, k_hbm, v_hbm, o_ref,
                        kbuf, vbuf, sems, *, n_heads, n_pages, page, past_len):
    s = pl.program_id(0)
    n_steps = pl.num_programs(0)
    ppb = MOBA_BLOCK // page
    n_slab = MOBA_TOPK * ppb

    def slab_copies(step, slot):
        b, h = step // n_heads, step % n_heads
        cps = []
        for t in range(n_slab):
            blk_id = idx_ref[step * MOBA_TOPK + t // ppb]
            pg = pt_ref[b * n_pages + blk_id * ppb + t % ppb]
            cps.append(pltpu.make_async_copy(k_hbm.at[pg, :, h, :], kbuf.at[slot, t], sems.at[slot, 0]))
            cps.append(pltpu.make_async_copy(v_hbm.at[pg, :, h, :], vbuf.at[slot, t], sems.at[slot, 1]))
        return cps

    @pl.when(s == 0)
    def _():
        for cp in slab_copies(s, 0):
            cp.start()

    @pl.when(s + 1 < n_steps)
    def _():
        for cp in slab_copies(s + 1, (s + 1) % 2):
            cp.start()

    slot = s % 2
    for cp in slab_copies(s, slot):
        cp.wait()

    q = q_ref[0]
    slope = sl_ref[0][:, 0:1]
    sub = lax.broadcasted_iota(jnp.int32, (page, 1), 0)
    s_self = jnp.sum(q * ks_ref[0], axis=1, keepdims=True)
    scores = []
    for t in range(n_slab):
        blk_id = idx_ref[s * MOBA_TOPK + t // ppb]
        kpos = blk_id * MOBA_BLOCK + (t % ppb) * page + sub
        dist = (past_len - kpos).astype(F32)
        scores.append(jnp.sum(kbuf[slot, t] * q, axis=1, keepdims=True) - slope * dist)
    mx = s_self
    for sc in scores:
        mx = jnp.maximum(mx, jnp.max(sc, axis=0, keepdims=True))
    p_self = jnp.exp(s_self - mx)
    den = p_self
    acc = p_self * vs_ref[0]
    for t, sc in enumerate(scores):
        p = jnp.exp(sc - mx)
        den = den + jnp.sum(p, axis=0, keepdims=True)
        acc = acc + jnp.sum(p * vbuf[slot, t], axis=0, keepdims=True)
    o_ref[0] = acc / den


def decode_attend(q, k_self, v_self, idx, cache_k, cache_v, page_table, slopes, *, past_len, name):
    n, _, hd = q.shape
    _, page, H, _ = cache_k.shape
    bsz, n_pages = page_table.shape
    n_slab = MOBA_TOPK * (MOBA_BLOCK // page)
    row = pl.BlockSpec((1, 1, hd), lambda s, idx_r, pt_r: (s, 0, 0))
    grid_spec = pltpu.PrefetchScalarGridSpec(
        num_scalar_prefetch=2, grid=(n,),
        in_specs=[row, row, row,
                  pl.BlockSpec((1, 1, LANES), lambda s, idx_r, pt_r: (s % H, 0, 0)),
                  pl.BlockSpec(memory_space=pl.ANY), pl.BlockSpec(memory_space=pl.ANY)],
        out_specs=row,
        scratch_shapes=[pltpu.VMEM((2, n_slab, page, hd), F32), pltpu.VMEM((2, n_slab, page, hd), F32),
                        pltpu.SemaphoreType.DMA((2, 2))])
    body = functools.partial(_decode_attend_body, n_heads=H, n_pages=n_pages, page=page, past_len=past_len)
    return pl.pallas_call(
        body, grid_spec=grid_spec, out_shape=jax.ShapeDtypeStruct((n, 1, hd), F32),
        compiler_params=_cparams(("arbitrary",)), name=name,
    )(idx, page_table.reshape(-1), q, k_self, v_self,
      jnp.broadcast_to(slopes.reshape(H, 1, 1), (H, 1, LANES)), cache_k, cache_v)


def kernel(x_prompt, x_sample, state_mlstm_C, state_mlstm_n, state_mlstm_m, state_conv, cache_k, cache_v, page_table, a_norm_g, a_w_in, a_b_if, a_head_g, a_w_out, kv_norm_g, w_kv, k_norm_g, b_norm_g, b_w_q, q_norm_g, b_w_o, f_norm_g, f_w_up, f_conv_w, f_conv_b, f_w_down):
    bsz, seq, d = x_prompt.shape
    dbsz, dseq, _ = x_sample.shape
    assert dseq == 1
    n_a, depth = a_w_in.shape[0], f_w_up.shape[0]
    n_b = depth - n_a
    dff = f_conv_w.shape[2]
    H = A_HEADS
    dk, dv = state_mlstm_C.shape[3], state_mlstm_C.shape[4]
    hd = d // B_HEADS
    past_len = page_table.shape[1] * cache_k.shape[1]
    s_main = 2 * H * dk + 2 * H * dv
    slopes = jnp.exp2(-8.0 * jnp.arange(1, B_HEADS + 1, dtype=F32) / B_HEADS)

    xp = x_prompt.reshape(bsz * seq, d)
    xs = x_sample.reshape(dbsz, d)
    Cp, Np, Mp, Cs, Ns, Ms, conv_p, conv_s = [], [], [], [], [], [], [], []

    dffp = -(-dff // FFN_TILE) * FFN_TILE
    pad = dffp - dff

    def conv_ffn(layer, xp, xs):
        w_up = jnp.pad(f_w_up[layer].reshape(d, 2, dff), ((0, 0), (0, 0), (0, pad))).reshape(d, 2 * dffp).astype(BF16)
        w_down = jnp.pad(f_w_down[layer], ((0, pad), (0, 0))).astype(BF16)
        cw = jnp.pad(f_conv_w[layer], ((0, 0), (0, pad)))
        cb = jnp.pad(f_conv_b[layer], (0, pad))
        act_p, gt_p = ffn_up(xp, f_norm_g[layer], w_up, cw, cb, tile=FFN_TILE, seq_len=seq,
                             name=f"ffn_up_p{layer}")
        xp = matmul_res(act_p, w_down, xp, name=f"ffn_down_p{layer}")
        conv_p.append(gt_p.reshape(bsz, -1, SUBLANES, dffp)[:, -1, SUBLANES - (CONV_W - 1):, :dff])
        buf = state_conv[layer]
        act_s, g_s = ffn_up(xs, f_norm_g[layer], f_w_up, f_conv_w[layer], f_conv_b[layer], tile=LANES,
                            w_layer=layer, prev=(buf[:, 0, :], buf[:, 1, :]), name=f"ffn_up_s{layer}")
        xs = matmul_res(act_s, f_w_down, xs, w_layer=layer, name=f"ffn_down_s{layer}")
        conv_s.append(jnp.stack([buf[:, 1, :], g_s], axis=1))
        return xp, xs

    for a in range(n_a):
        w_main = a_w_in[a][:, :s_main].astype(BF16)
        w_gate = jnp.zeros((d, LANES), BF16).at[:, :2 * H].set(a_w_in[a][:, s_main:].astype(BF16))
        w_out = a_w_out[a].astype(BF16)
        proj_p, gates_p = norm_matmul(xp, a_norm_g[a], w_main, w_side=w_gate, name=f"mlstm_in_p{a}")
        h_p, c1, n1, m1 = mlstm(
            proj_p.reshape(bsz, seq, s_main), gates_p.reshape(bsz, seq, LANES), a_b_if[a], a_head_g[a],
            jnp.zeros((bsz, H, dk, dv), F32), jnp.zeros((bsz, H, dk), F32), jnp.zeros((bsz, H), F32),
            n_valid=A_CHUNK, name=f"mlstm_p{a}")
        xp = matmul_res(h_p.reshape(bsz * seq, H * dv), w_out, xp, name=f"mlstm_out_p{a}")
        w_gate_s = jnp.zeros((d, LANES), F32).at[:, :2 * H].set(a_w_in[a][:, s_main:])
        proj_s, gates_s = norm_matmul(xs, a_norm_g[a], a_w_in, w_layer=a, n_cols=s_main, w_side=w_gate_s,
                                      name=f"mlstm_in_s{a}")
        proj_s = jnp.zeros((dbsz, A_CHUNK, s_main), F32).at[:, 0, :].set(proj_s)
        gates_s = jnp.zeros((dbsz, A_CHUNK, LANES), F32).at[:, 0, :].set(gates_s)
        h_s, c2, n2, m2 = mlstm(proj_s, gates_s, a_b_if[a], a_head_g[a],
                                state_mlstm_C[a], state_mlstm_n[a], state_mlstm_m[a],
                                n_valid=1, precise=True, name=f"mlstm_s{a}")
        xs = matmul_res(h_s[:, 0, :], a_w_out, xs, w_layer=a, name=f"mlstm_out_s{a}")
        Cp.append(c1), Np.append(n1), Mp.append(m1.reshape(bsz, H))
        Cs.append(c2), Ns.append(n2), Ms.append(m2.reshape(dbsz, H))
        xp, xs = conv_ffn(a, xp, xs)

    w_kv_b = w_kv.astype(BF16)
    kv_p, k_p4, v_p4 = norm_matmul(xp, kv_norm_g, w_kv_b, head_g=k_norm_g, n_norm_cols=d, heads_out=True,
                                   name="shared_kv_p")
    kv_s = norm_matmul(xs, kv_norm_g, w_kv, head_g=k_norm_g, n_norm_cols=d, name="shared_kv_s")
    k_s, v_s = kv_s[:, :d], kv_s[:, d:]
    kmean_s = paged_block_means(cache_k, page_table)
    kv_p3 = kv_p.reshape(bsz, seq, 2 * d)

    for bl in range(n_b):
        w_q = b_w_q[bl].astype(BF16)
        w_o = b_w_o[bl].astype(BF16)
        q_p = norm_matmul(xp, b_norm_g[bl], w_q, head_g=q_norm_g[bl], n_norm_cols=d,
                          head_scale=hd ** -0.5, name=f"moba_q_p{bl}")
        o_p = moba_seq(q_p.reshape(bsz, seq, d), kv_p3, slopes, name=f"moba_attn_p{bl}")
        xp = matmul_res(o_p.reshape(bsz * seq, d), w_o, xp, name=f"moba_out_p{bl}")
        q_s = norm_matmul(xs, b_norm_g[bl], b_w_q, w_layer=bl, head_g=q_norm_g[bl], n_norm_cols=d,
                          head_scale=hd ** -0.5, name=f"moba_q_s{bl}")
        idx = decode_select(q_s.reshape(dbsz, B_HEADS, hd), kmean_s)[:, :, :MOBA_TOPK].reshape(-1)
        o_s = decode_attend(q_s.reshape(dbsz * B_HEADS, 1, hd), k_s.reshape(dbsz * B_HEADS, 1, hd),
                            v_s.reshape(dbsz * B_HEADS, 1, hd), idx, cache_k, cache_v, page_table,
                            slopes, past_len=past_len, name=f"decode_attend{bl}")
        xs = matmul_res(o_s.reshape(dbsz, d), b_w_o, xs, w_layer=bl, name=f"moba_out_s{bl}")
        xp, xs = conv_ffn(n_a + bl, xp, xs)

    return (xp.reshape(bsz, seq, d), xs.reshape(dbsz, 1, d),
            jnp.stack(Cp), jnp.stack(Np), jnp.stack(Mp),
            jnp.stack(Cs), jnp.stack(Ns), jnp.stack(Ms),
            jnp.stack(conv_p), jnp.stack(conv_s),
            k_p4.reshape(bsz, seq, B_HEADS, hd), v_p4.reshape(bsz, seq, B_HEADS, hd),
            k_s.reshape(dbsz, 1, B_HEADS, hd), v_s.reshape(dbsz, 1, B_HEADS, hd))
```

```python
import functools
import math

import jax
import jax.numpy as jnp
from jax import lax
from jax.experimental import pallas as pl
from jax.experimental.pallas import tpu as pltpu

F32 = jnp.float32
BF16 = jnp.bfloat16

EPS = 1e-6
NEG = -1e30
LANES = 128
SUBLANES = 8
MXU_N = 256
VMEM_LIMIT = 56 * 1024 * 1024

A_HEADS = 8
A_CHUNK = 128
B_HEADS = 16
MOBA_BLOCK = 256
MOBA_TOPK = 3
MOBA_SEL_ROWS = 16
MOBA_HEADS_PER_STEP = 2
CONV_W = 3
FFN_TILE = 512


def _cparams(sem):
    return pltpu.CompilerParams(dimension_semantics=sem, vmem_limit_bytes=VMEM_LIMIT)


def _row_tile(m):
    return min(m, 1024)


def _col_tile(n, cap=1024):
    for t in (1024, 512, 256, 128):
        if t <= cap and n % t == 0:
            return t
    return n


def _split2(a):
    hi = a.astype(BF16)
    return hi, (a - hi.astype(F32)).astype(BF16)


def _mm(a, w):
    if w.dtype == BF16:
        return jnp.dot(a.astype(BF16), w, preferred_element_type=F32)
    (a_hi, a_lo), (w_hi, w_lo) = _split2(a), _split2(w)
    return (jnp.dot(a_hi, w_hi, preferred_element_type=F32) + jnp.dot(a_lo, w_hi, preferred_element_type=F32)
            + jnp.dot(a_hi, w_lo, preferred_element_type=F32))


def _bf16_round(x):
    return x.astype(BF16).astype(F32)


def _norm_matmul_body(*refs, n_norm_tiles, n_tiles, head_scale, has_side, heads_out):
    it = iter(refs)
    x_ref, g_ref, w_ref = next(it), next(it), next(it)
    hg_ref = next(it) if n_norm_tiles else None
    w2_ref = next(it) if has_side else None
    o_ref = next(it)
    o2_ref = next(it) if has_side else None
    ha_ref, hb_ref = (next(it), next(it)) if heads_out else (None, None)
    xn_ref = next(it)
    j = pl.program_id(1)

    @pl.when(j == 0)
    def _():
        x = x_ref[...]
        ms = jnp.mean(x * x, axis=-1, keepdims=True)
        xn = (x * lax.rsqrt(ms + EPS) * g_ref[...]).astype(xn_ref.dtype)
        xn_ref[...] = xn
        if has_side:
            o2_ref[...] = _mm(xn, w2_ref[...])

    acc = _mm(xn_ref[...], w_ref[...])

    def headnorm(a):
        parts = []
        for c in range(a.shape[1] // LANES):
            p = a[:, c * LANES:(c + 1) * LANES]
            ms = jnp.mean(p * p, axis=-1, keepdims=True)
            p = p * lax.rsqrt(ms + EPS) * hg_ref[...]
            if head_scale != 1.0:
                p = p * head_scale
            parts.append(p)
        return jnp.concatenate(parts, axis=1) if len(parts) > 1 else parts[0]

    def store(r, h_ref):
        o_ref[...] = r.astype(o_ref.dtype)
        if h_ref is not None:
            for h in range(r.shape[1] // LANES):
                h_ref[:, h, :] = r[:, h * LANES:(h + 1) * LANES]

    if n_norm_tiles == 0:
        store(acc, None)
    elif n_norm_tiles >= n_tiles:
        store(headnorm(acc), None)
    else:
        @pl.when(j < n_norm_tiles)
        def _():
            store(headnorm(acc), ha_ref)

        @pl.when(j >= n_norm_tiles)
        def _():
            store(acc, hb_ref)


def _weight_spec(w, layer, k, tn, col0=0):
    if w.ndim == 2:
        return pl.BlockSpec((k, tn), lambda i, j: (0, col0 + j))
    return pl.BlockSpec((None, k, tn), lambda i, j: (layer, 0, col0 + j))


def norm_matmul(x, g, w, *, w_layer=None, n_cols=None, head_g=None, n_norm_cols=0, head_scale=1.0,
                w_side=None, heads_out=False, name):
    m, k = x.shape
    n = w.shape[-1] if n_cols is None else n_cols
    tm, tn = _row_tile(m), _col_tile(n, cap=1024 if w.dtype == BF16 else 512)
    if heads_out:
        tm = min(tm, 512)
    n_tiles = n // tn
    assert n_norm_cols % tn == 0
    n_norm_tiles = n_norm_cols // tn
    assert not heads_out or (2 * n_norm_tiles == n_tiles and w_side is None and tn // LANES >= SUBLANES)
    in_specs = [pl.BlockSpec((tm, k), lambda i, j: (i, 0)),
                pl.BlockSpec((1, k), lambda i, j: (0, 0)),
                _weight_spec(w, w_layer, k, tn)]
    args = [x, g.reshape(1, k), w]
    if n_norm_tiles:
        in_specs.append(pl.BlockSpec((1, LANES), lambda i, j: (0, 0)))
        args.append(head_g.reshape(1, LANES))
    out_shape = [jax.ShapeDtypeStruct((m, n), F32)]
    out_specs = [pl.BlockSpec((tm, tn), lambda i, j: (i, j))]
    if w_side is not None:
        in_specs.append(pl.BlockSpec((k, LANES), lambda i, j: (0, 0)))
        args.append(w_side)
        out_shape.append(jax.ShapeDtypeStruct((m, LANES), F32))
        out_specs.append(pl.BlockSpec((tm, LANES), lambda i, j: (i, 0)))
    if heads_out:
        hpt = tn // LANES
        half = n_tiles // 2
        hshape = jax.ShapeDtypeStruct((m, half * hpt, LANES), F32)
        out_shape += [hshape, hshape]
        out_specs += [pl.BlockSpec((tm, hpt, LANES), lambda i, j: (i, jnp.minimum(j, half - 1), 0)),
                      pl.BlockSpec((tm, hpt, LANES), lambda i, j: (i, jnp.maximum(j - half, 0), 0))]
    body = functools.partial(_norm_matmul_body, n_norm_tiles=n_norm_tiles, n_tiles=n_tiles,
                             head_scale=head_scale, has_side=w_side is not None, heads_out=heads_out)
    outs = pl.pallas_call(
        body, grid=(m // tm, n_tiles), in_specs=in_specs, out_specs=out_specs, out_shape=out_shape,
        scratch_shapes=[pltpu.VMEM((tm, k), w.dtype)],
        compiler_params=_cparams(("parallel", "arbitrary")), name=name)(*args)
    return outs if len(outs) > 1 else outs[0]


def _matmul_res_body(a_ref, w_ref, r_ref, o_ref):
    o_ref[...] = r_ref[...] + _mm(a_ref[...], w_ref[...])


def matmul_res(a, w, res, *, w_layer=None, name):
    m, k = a.shape
    n = w.shape[-1]
    tm, tn = _row_tile(m), _col_tile(n, cap=512 if w.dtype == BF16 else 256)
    return pl.pallas_call(
        _matmul_res_body, grid=(m // tm, n // tn),
        in_specs=[pl.BlockSpec((tm, k), lambda i, j: (i, 0)),
                  _weight_spec(w, w_layer, k, tn),
                  pl.BlockSpec((tm, tn), lambda i, j: (i, j))],
        out_specs=pl.BlockSpec((tm, tn), lambda i, j: (i, j)),
        out_shape=jax.ShapeDtypeStruct((m, n), F32),
        compiler_params=_cparams(("parallel", "arbitrary")), name=name)(a, w, res)


def _ffn_up_body(*refs, seq_mode, blocks_per_seq):
    if seq_mode:
        x_ref, xh_ref, g_ref, wu_ref, wg_ref, cw_ref, cb_ref, act_ref, gt_ref, xn_ref, xhn_ref = refs
    else:
        x_ref, p2_ref, p1_ref, g_ref, wu_ref, wg_ref, cw_ref, cb_ref, act_ref, gt_ref, xn_ref = refs
    i, j = pl.program_id(0), pl.program_id(1)

    def norm(x):
        ms = jnp.mean(x * x, axis=-1, keepdims=True)
        return (x * lax.rsqrt(ms + EPS) * g_ref[...]).astype(xn_ref.dtype)

    @pl.when(j == 0)
    def _():
        xn_ref[...] = norm(x_ref[...])
        if seq_mode:
            xhn_ref[...] = norm(xh_ref[...])

    xn = xn_ref[...]
    tm = xn.shape[0]
    tw = wu_ref.shape[1]
    cw_ = min(tw, MXU_N)
    for c in range(tw // cw_):
        cs = slice(c * cw_, (c + 1) * cw_)
        u = _mm(xn, wu_ref[:, cs])
        g = _mm(xn, wg_ref[:, cs])
        if seq_mode:
            gh = _mm(xhn_ref[...], wg_ref[:, cs])
            gh = jnp.where(i % blocks_per_seq == 0, 0.0, gh)
            prev1, prev2 = gh[SUBLANES - 1:SUBLANES, :], gh[SUBLANES - 2:SUBLANES - 1, :]
            row = lax.broadcasted_iota(jnp.int32, (SUBLANES, cw_), 0)
            r1, r2 = pltpu.roll(g, 1, 0), pltpu.roll(g, 2, 0)
            g1 = jnp.concatenate([jnp.where(row == 0, prev1, r1[:SUBLANES])] + ([r1[SUBLANES:]] if tm > SUBLANES else []), axis=0)
            g2 = jnp.concatenate([jnp.where(row == 0, prev2, jnp.where(row == 1, prev1, r2[:SUBLANES]))]
                                 + ([r2[SUBLANES:]] if tm > SUBLANES else []), axis=0)
            gt_ref[0, :, cs] = g[tm - SUBLANES:, :]
        else:
            g1, g2 = p1_ref[:, cs], p2_ref[:, cs]
            gt_ref[:, cs] = g
        gc = cb_ref[:, cs] + g2 * cw_ref[0:1, cs] + g1 * cw_ref[1:2, cs] + g * cw_ref[2:3, cs]
        act = gc * (1.0 / (1.0 + jnp.exp(-gc))) * u
        act_ref[:, cs] = act.astype(act_ref.dtype)


def ffn_up(x, g, w_u, w_g, conv_w, conv_b, *, tile, g_col0=0, w_layer=None, seq_len=None, prev=None, name):
    m, k = x.shape
    dff = conv_w.shape[1]
    tw = tile
    assert dff % tw == 0
    nt = dff // tw
    wdt = w_u.dtype
    tm = _row_tile(m)
    seq_mode = prev is None
    x_spec = pl.BlockSpec((tm, k), lambda i, j: (i, 0))
    common = [pl.BlockSpec((1, k), lambda i, j: (0, 0)),
              _weight_spec(w_u, w_layer, k, tw),
              _weight_spec(w_g, w_layer, k, tw, col0=g_col0),
              pl.BlockSpec((CONV_W, tw), lambda i, j: (0, j)),
              pl.BlockSpec((1, tw), lambda i, j: (0, j))]
    cargs = [g.reshape(1, k), w_u, w_g, conv_w, conv_b.reshape(1, dff)]
    act_spec = pl.BlockSpec((tm, tw), lambda i, j: (i, j))
    if seq_mode:
        tpb = tm // SUBLANES
        in_specs = [x_spec, pl.BlockSpec((SUBLANES, k), lambda i, j: (jnp.maximum(i * tpb - 1, 0), 0))] + common
        args = [x, x] + cargs
        out_shape = [jax.ShapeDtypeStruct((m, dff), wdt), jax.ShapeDtypeStruct((m // tm, SUBLANES, dff), F32)]
        out_specs = [act_spec, pl.BlockSpec((1, SUBLANES, tw), lambda i, j: (i, 0, j))]
        scratch = [pltpu.VMEM((tm, k), wdt), pltpu.VMEM((SUBLANES, k), wdt)]
        bps = seq_len // tm
    else:
        pspec = pl.BlockSpec((tm, tw), lambda i, j: (i, j))
        in_specs = [x_spec, pspec, pspec] + common
        args = [x, prev[0], prev[1]] + cargs
        out_shape = [jax.ShapeDtypeStruct((m, dff), wdt), jax.ShapeDtypeStruct((m, dff), F32)]
        out_specs = [act_spec, pl.BlockSpec((tm, tw), lambda i, j: (i, j))]
        scratch = [pltpu.VMEM((tm, k), wdt)]
        bps = 1
    body = functools.partial(_ffn_up_body, seq_mode=seq_mode, blocks_per_seq=bps)
    return pl.pallas_call(
        body, grid=(m // tm, nt), in_specs=in_specs, out_specs=out_specs, out_shape=out_shape,
        scratch_shapes=scratch, compiler_params=_cparams(("parallel", "arbitrary")), name=name)(*args)


def _log_sigmoid(x):
    return jnp.minimum(x, 0.0) - jnp.log(1.0 + jnp.exp(-jnp.abs(x)))


def _mlstm_body(proj_ref, gc_ref, gr_ref, bc_ref, br_ref, hg_ref, c0_ref, n0_ref, m0_ref,
                h_ref, c_ref, n_ref, m_ref, *, n_valid, dk, dv, precise):
    c_idx = pl.program_id(1)
    L = proj_ref.shape[1]
    H = A_HEADS

    @pl.when(c_idx == 0)
    def _():
        c_ref[...] = c0_ref[...]
        n_ref[...] = n0_ref[...]
        m_ref[...] = m0_ref[...]

    gates_c = gc_ref[0] + bc_ref[...]
    gates_r = gr_ref[0] + br_ref[...]
    row = lax.broadcasted_iota(jnp.int32, (L, L), 0)
    col = lax.broadcasted_iota(jnp.int32, (L, L), 1)
    tri = col <= row
    if n_valid < L:
        vc = lax.broadcasted_iota(jnp.int32, (L, 1), 0) < n_valid
        vr = lax.broadcasted_iota(jnp.int32, (1, L), 1) < n_valid

    s1, s2, s3 = H * dk, 2 * H * dk, 2 * H * dk + H * dv
    heads = range(H)

    def gate_stage(h):
        ig_c = gates_c[:, h:h + 1]
        lf_c = _log_sigmoid(gates_c[:, H + h:H + h + 1])
        ig_r = gates_r[h:h + 1, :]
        lf_r = _log_sigmoid(gates_r[H + h:H + h + 1, :])
        if n_valid < L:
            ig_c, lf_c = jnp.where(vc, ig_c, NEG), jnp.where(vc, lf_c, 0.0)
            ig_r, lf_r = jnp.where(vr, ig_r, NEG), jnp.where(vr, lf_r, 0.0)
        b_c = jnp.sum(jnp.where(tri, lf_r, 0.0), axis=1, keepdims=True)
        b_r = jnp.sum(jnp.where(row <= col, lf_c, 0.0), axis=0, keepdims=True)
        m_prev = m_ref[0, h:h + 1, :]
        logd = jnp.where(tri, b_c - b_r + ig_r, NEG)
        m_t = jnp.maximum(b_c + m_prev, jnp.max(logd, axis=1, keepdims=True))
        d = jnp.exp(logd - m_t)
        inter = jnp.exp(b_c + m_prev - m_t)
        m_new = m_t[L - 1:L, :]
        b_last = b_c[L - 1:L, :]
        w_s = jnp.exp(b_last - b_c + ig_c - m_new)
        decay = jnp.exp(b_last + m_prev - m_new)
        return dict(m_t=m_t, d=d, inter=inter, m_new=m_new, w_s=w_s, decay=decay)

    nn, nt, tn = (((1,), (0,)), ((), ())), (((1,), (1,)), ((), ())), (((0,), (0,)), ((), ()))

    def mm(a, b, dims):
        if not precise:
            return lax.dot_general(a.astype(BF16), b.astype(BF16), dims, preferred_element_type=F32)
        (a_hi, a_lo), (b_hi, b_lo) = _split2(a), _split2(b)
        return (lax.dot_general(a_hi, b_hi, dims, preferred_element_type=F32)
                + lax.dot_general(a_lo, b_hi, dims, preferred_element_type=F32)
                + lax.dot_general(a_hi, b_lo, dims, preferred_element_type=F32))

    rnd = (lambda x: x) if precise else _bf16_round

    def state_dots(h, gt):
        q = proj_ref[0, :, h * dk:(h + 1) * dk] * (dk ** -0.5)
        k = proj_ref[0, :, s1 + h * dk:s1 + (h + 1) * dk]
        v = proj_ref[0, :, s2 + h * dv:s2 + (h + 1) * dv]
        v = v if precise else v.astype(BF16)
        kw = k * gt["w_s"]
        return dict(qn=rnd(q), kn=rnd(k) * rnd(gt["w_s"]), v=v,
                    s=mm(q, k, nt), qc=mm(q, c_ref[0, h], nn), kv=mm(kw, v, tn))

    def value_dot(gt, st):
        w = st["s"] * gt["d"]
        return w, mm(w, st["v"], nn)

    def finish(h, gt, st, w, pv):
        n_prev = n_ref[0, h:h + 1, :]
        num = gt["inter"] * st["qc"] + pv
        den = gt["inter"] * jnp.sum(st["qn"] * rnd(n_prev), axis=1, keepdims=True) \
            + jnp.sum(w, axis=1, keepdims=True)
        hh = num / jnp.maximum(jnp.abs(den), jnp.exp(-gt["m_t"]))
        c_ref[0, h] = gt["decay"] * c_ref[0, h] + st["kv"]
        n_ref[0, h:h + 1, :] = gt["decay"] * n_prev + jnp.sum(st["kn"], axis=0, keepdims=True)
        m_ref[0, h:h + 1, :] = gt["m_new"]
        og = proj_ref[0, :, s3 + h * dv:s3 + (h + 1) * dv]
        hn = hh * lax.rsqrt(jnp.mean(hh * hh, axis=1, keepdims=True) + EPS)
        hn = hn * hg_ref[:, h * dv:(h + 1) * dv] * (1.0 / (1.0 + jnp.exp(-og)))
        h_ref[0, :, h * dv:(h + 1) * dv] = hn.astype(h_ref.dtype)

    gts = [gate_stage(h) for h in heads]
    sts = [state_dots(h, gts[h]) for h in heads]
    wps = [value_dot(gts[h], sts[h]) for h in heads]
    for h in heads:
        finish(h, gts[h], sts[h], *wps[h])


def mlstm(proj, gates, b_if, head_g, c0, n0, m0, *, n_valid, precise=False, name):
    bsz, t, _ = proj.shape
    H = A_HEADS
    dk, dv = c0.shape[2], c0.shape[3]
    L = min(A_CHUNK, t)
    nc = t // L
    gates_r = jnp.swapaxes(gates[:, :, :2 * H], 1, 2)
    bias_c = jnp.zeros((1, LANES), F32).at[0, :2 * H].set(b_if)
    bias_r = b_if.reshape(2 * H, 1)
    body = functools.partial(_mlstm_body, n_valid=n_valid, dk=dk, dv=dv, precise=precise)
    return pl.pallas_call(
        body, grid=(bsz, nc),
        in_specs=[pl.BlockSpec((1, L, proj.shape[2]), lambda b, c: (b, c, 0)),
                  pl.BlockSpec((1, L, LANES), lambda b, c: (b, c, 0)),
                  pl.BlockSpec((1, 2 * H, L), lambda b, c: (b, 0, c)),
                  pl.BlockSpec((1, LANES), lambda b, c: (0, 0)),
                  pl.BlockSpec((2 * H, 1), lambda b, c: (0, 0)),
                  pl.BlockSpec((1, H * dv), lambda b, c: (0, 0)),
                  pl.BlockSpec((1, H, dk, dv), lambda b, c: (b, 0, 0, 0)),
                  pl.BlockSpec((1, H, dk), lambda b, c: (b, 0, 0)),
                  pl.BlockSpec((1, H, 1), lambda b, c: (b, 0, 0))],
        out_specs=[pl.BlockSpec((1, L, H * dv), lambda b, c: (b, c, 0)),
                   pl.BlockSpec((1, H, dk, dv), lambda b, c: (b, 0, 0, 0)),
                   pl.BlockSpec((1, H, dk), lambda b, c: (b, 0, 0)),
                   pl.BlockSpec((1, H, 1), lambda b, c: (b, 0, 0))],
        out_shape=[jax.ShapeDtypeStruct((bsz, t, H * dv), F32 if precise else BF16),
                   jax.ShapeDtypeStruct((bsz, H, dk, dv), F32),
                   jax.ShapeDtypeStruct((bsz, H, dk), F32),
                   jax.ShapeDtypeStruct((bsz, H, 1), F32)],
        compiler_params=_cparams(("parallel", "arbitrary")), name=name,
    )(proj, gates, gates_r, bias_c, bias_r, head_g.reshape(1, H * dv), c0, n0, m0.reshape(bsz, H, 1))


def _moba_seq_body(q_ref, k_ref, v_ref, sl_ref, o_ref, kaug_ref, vt_ref, km_ref, mask_ref, *, nb, G):
    qi = pl.program_id(2)
    blk = MOBA_BLOCK
    hd = LANES

    @pl.when(qi == 0)
    def _():
        kk = lax.broadcasted_iota(jnp.int32, (blk, blk), 0)
        qq = lax.broadcasted_iota(jnp.int32, (blk, blk), 1)
        causal = jnp.where(kk <= qq, 0.0, NEG)
        mask_ref[0, :blk, :] = causal
        mask_ref[0, blk:, :] = jnp.full((blk, blk), NEG, F32)
        mask_ref[1, :blk, :] = jnp.zeros((blk, blk), F32)
        mask_ref[1, blk:, :] = causal
        lane = lax.broadcasted_iota(jnp.int32, (blk, hd), 1)
        row = lax.broadcasted_iota(jnp.int32, (blk, hd), 0)
        ones_row = jnp.where(lax.broadcasted_iota(jnp.int32, (MOBA_SEL_ROWS, blk), 0) == 0, 1.0, 0.0)
        km_ref[...] = jnp.zeros_like(km_ref)
        for g in range(G):
            slope2 = sl_ref[0, g:g + 1, :]
            for n in range(nb):
                j, r0 = n // 2, (n % 2) * blk
                kf = k_ref[0, n * blk:(n + 1) * blk, g * hd:(g + 1) * hd]
                a = (row + n * blk).astype(F32) * slope2
                a_hi = a.astype(BF16)
                a_r = a - a_hi.astype(F32)
                a_mid = a_r.astype(BF16)
                a_lo = (a_r - a_mid.astype(F32)).astype(BF16)
                aug = jnp.where(lane == n, 1.0, 0.0).astype(BF16)
                aug = jnp.where(lane == MOBA_SEL_ROWS, a_hi, aug)
                aug = jnp.where(lane == MOBA_SEL_ROWS + 1, a_mid, aug)
                aug = jnp.where(lane == MOBA_SEL_ROWS + 2, a_lo, aug)
                kaug_ref[g, j, r0:r0 + blk, :hd] = kf.astype(BF16)
                kaug_ref[g, j, r0:r0 + blk, hd:] = aug
                km_ref[g, n:n + 1, :] = jnp.mean(kf, axis=0, keepdims=True)
                vt = v_ref[0, n * blk:(n + 1) * blk, g * hd:(g + 1) * hd].T
                vt_ref[g, j, :hd, r0:r0 + blk] = vt.astype(BF16)
                vt_ref[g, j, hd:, r0:r0 + blk] = ones_row.astype(BF16)

    rowb = lax.broadcasted_iota(jnp.int32, (MOBA_SEL_ROWS, blk), 0)
    ones_rows = jnp.where(rowb < 3, 1.0, 0.0).astype(BF16)
    qaugs = []
    for g in range(G):
        qt = q_ref[0, :, g * hd:(g + 1) * hd].T
        q_hi = qt.astype(BF16)
        gate = jnp.dot(km_ref[g].astype(BF16), q_hi, preferred_element_type=F32)
        cnt = jnp.zeros(gate.shape, jnp.int32)
        for m in range(nb - 1):
            gm = gate[m:m + 1, :]
            beats = (gm > gate) | ((gm == gate) & (m < rowb))
            cnt = cnt + jnp.where(beats, jnp.where(m < qi, 1, 0), 0)
        keep = (rowb >= qi) | (cnt < MOBA_TOPK)
        selb = jnp.where(keep, 0.0, NEG).astype(BF16)
        qaugs.append(jnp.concatenate(
            [q_hi, selb, ones_rows, jnp.zeros((hd - 2 * MOBA_SEL_ROWS, blk), BF16)], axis=0))

    def pairs_step(js, carry, masks):
        ss = [[jnp.dot(kaug_ref[g, j], qaugs[g], preferred_element_type=F32) for g in range(G)]
              for j in js]
        m_run = [carry[g][0] for g in range(G)]
        acc = [carry[g][1] for g in range(G)]
        for t, j in enumerate(js):
            s_t = ss[t] if masks[t] is None else [s + masks[t] for s in ss[t]]
            m_new = [jnp.maximum(m_run[g], jnp.max(s_t[g], axis=0, keepdims=True)) for g in range(G)]
            ps = [jnp.exp(s_t[g] - m_new[g]).astype(BF16) for g in range(G)]
            pvs = [jnp.dot(vt_ref[g, j], ps[g], preferred_element_type=F32) for g in range(G)]
            acc = [jnp.exp(m_run[g] - m_new[g]) * acc[g] + pvs[g] for g in range(G)]
            m_run = m_new
        return tuple((m_run[g], acc[g]) for g in range(G))

    init = tuple((jnp.full((1, blk), -jnp.inf, F32), jnp.zeros((hd + MOBA_SEL_ROWS, blk), F32))
                 for _ in range(G))
    last = (qi + 2) // 2 - 1
    carry = lax.fori_loop(0, last // 2, lambda i, c: pairs_step([2 * i, 2 * i + 1], c, [None, None]), init)
    carry = lax.cond(
        last % 2 == 1,
        lambda c: pairs_step([last - 1, last], c, [None, mask_ref[qi % 2]]),
        lambda c: pairs_step([last], c, [mask_ref[qi % 2]]),
        carry)
    for g in range(G):
        _, acc = carry[g]
        o = (acc[:hd] / acc[hd:hd + 1]).T
        o_ref[0, :, g * hd:(g + 1) * hd] = o.astype(o_ref.dtype)


def moba_seq(q, kv, slopes, *, name):
    bsz, t, d = q.shape
    hd = LANES
    H = d // hd
    G = MOBA_HEADS_PER_STEP
    blk = MOBA_BLOCK
    nb = t // blk
    assert H % G == 0 and nb % 2 == 0 and nb <= MOBA_SEL_ROWS
    body = functools.partial(_moba_seq_body, nb=nb, G=G)
    return pl.pallas_call(
        body, grid=(bsz, H // G, nb),
        in_specs=[pl.BlockSpec((1, blk, G * hd), lambda b, h, i: (b, i, h)),
                  pl.BlockSpec((1, t, G * hd), lambda b, h, i: (b, 0, h)),
                  pl.BlockSpec((1, t, G * hd), lambda b, h, i: (b, 0, H // G + h)),
                  pl.BlockSpec((1, G, LANES), lambda b, h, i: (h, 0, 0))],
        out_specs=pl.BlockSpec((1, blk, G * hd), lambda b, h, i: (b, i, h)),
        out_shape=jax.ShapeDtypeStruct((bsz, t, d), BF16),
        scratch_shapes=[pltpu.VMEM((G, nb // 2, 2 * blk, 2 * hd), BF16),
                        pltpu.VMEM((G, nb // 2, hd + MOBA_SEL_ROWS, 2 * blk), BF16),
                        pltpu.VMEM((G, MOBA_SEL_ROWS, hd), F32),
                        pltpu.VMEM((2, 2 * blk, blk), F32)],
        compiler_params=_cparams(("parallel", "parallel", "arbitrary")), name=name,
    )(q, kv, kv, jnp.broadcast_to(slopes.reshape(H // G, G, 1), (H // G, G, LANES)))


def _page_mean_body(pt_ref, *refs, ppb):
    k_refs, o_ref = refs[:-1], refs[-1]
    for t in range(len(k_refs) // ppb):
        s = jnp.sum(k_refs[t * ppb][0], axis=0)
        for r in k_refs[t * ppb + 1:(t + 1) * ppb]:
            s = s + jnp.sum(r[0], axis=0)
        o_ref[0, t] = s * (1.0 / MOBA_BLOCK)


def paged_block_means(cache_k, page_table):
    _, page, H, hd = cache_k.shape
    bsz, n_pages = page_table.shape
    ppb = MOBA_BLOCK // page
    nblk = n_pages // ppb
    bps = 2 if nblk % 2 == 0 else 1
    pps = bps * ppb

    def page_spec(t):
        return pl.BlockSpec((1, page, H, hd), lambda b, n, pt: (pt[b, pps * n + t], 0, 0, 0))

    grid_spec = pltpu.PrefetchScalarGridSpec(
        num_scalar_prefetch=1, grid=(bsz, nblk // bps),
        in_specs=[page_spec(t) for t in range(pps)],
        out_specs=pl.BlockSpec((1, bps, H, hd), lambda b, n, pt: (b, n, 0, 0)))
    return pl.pallas_call(
        functools.partial(_page_mean_body, ppb=ppb), grid_spec=grid_spec,
        out_shape=jax.ShapeDtypeStruct((bsz, nblk, H, hd), F32),
        compiler_params=_cparams(("parallel", "arbitrary")), name="paged_block_means",
    )(page_table, *([cache_k] * pps))


def _decode_select_body(q_ref, km_ref, idx_ref, *, nblk):
    q = q_ref[0]
    lane = lax.broadcasted_iota(jnp.int32, (q.shape[0], LANES), 1)
    gate = jnp.full((q.shape[0], LANES), NEG, F32)
    for n in range(nblk):
        gn = jnp.sum(q * km_ref[0, n], axis=1, keepdims=True)
        gate = jnp.where(lane == n, gn, gate)
    out = jnp.zeros((q.shape[0], LANES), jnp.int32)
    for j in range(MOBA_TOPK):
        mx = jnp.max(gate, axis=1, keepdims=True)
        am = jnp.min(jnp.where(gate == mx, lane, LANES), axis=1, keepdims=True)
        out = jnp.where(lane == j, am, out)
        gate = jnp.where(lane == am, -jnp.inf, gate)
    idx_ref[0] = out


def decode_select(q, kmean):
    bsz, H, hd = q.shape
    nblk = kmean.shape[1]
    assert MOBA_TOPK <= nblk <= LANES
    return pl.pallas_call(
        functools.partial(_decode_select_body, nblk=nblk), grid=(bsz,),
        in_specs=[pl.BlockSpec((1, H, hd), lambda b: (b, 0, 0)),
                  pl.BlockSpec((1, nblk, H, hd), lambda b: (b, 0, 0, 0))],
        out_specs=pl.BlockSpec((1, H, LANES), lambda b: (b, 0, 0)),
        out_shape=jax.ShapeDtypeStruct((bsz, H, LANES), jnp.int32),
        compiler_params=_cparams(("parallel",)), name="decode_select")(q, kmean)


def _decode_attend_body(idx_ref, pt_ref, q_ref, ks_ref, vs_ref, sl_ref, k_hbm, v_hbm, o_ref,
                        kbuf, vbuf, sems, *, n_heads, n_pages, page, past_len):
    s = pl.program_id(0)
    n_steps = pl.num_programs(0)
    ppb = MOBA_BLOCK // page
    n_slab = MOBA_TOPK * ppb

    def slab_copies(step, slot):
        b, h = step // n_heads, step % n_heads
        cps = []
        for t in range(n_slab):
            blk_id = idx_ref[step * MOBA_TOPK + t // ppb]
            pg = pt_ref[b * n_pages + blk_id * ppb + t % ppb]
            cps.append(pltpu.make_async_copy(k_hbm.at[pg, :, h, :], kbuf.at[slot, t], sems.at[slot, 0]))
            cps.append(pltpu.make_async_copy(v_hbm.at[pg, :, h, :], vbuf.at[slot, t], sems.at[slot, 1]))
        return cps

    @pl.when(s == 0)
    def _():
        for cp in slab_copies(s, 0):
            cp.start()

    @pl.when(s + 1 < n_steps)
    def _():
        for cp in slab_copies(s + 1, (s + 1) % 2):
            cp.start()

    slot = s % 2
    for cp in slab_copies(s, slot):
        cp.wait()

    q = q_ref[0]
    slope = sl_ref[0][:, 0:1]
    sub = lax.broadcasted_iota(jnp.int32, (page, 1), 0)
    s_self = jnp.sum(q * ks_ref[0], axis=1, keepdims=True)
    scores = []
    for t in range(n_slab):
        blk_id = idx_ref[s * MOBA_TOPK + t // ppb]
        kpos = blk_id * MOBA_BLOCK + (t % ppb) * page + sub
        dist = (past_len - kpos).astype(F32)
        scores.append(jnp.sum(kbuf[slot, t] * q, axis=1, keepdims=True) - slope * dist)
    mx = s_self
    for sc in scores:
        mx = jnp.maximum(mx, jnp.max(sc, axis=0, keepdims=True))
    p_self = jnp.exp(s_self - mx)
    den = p_self
    acc = p_self * vs_ref[0]
    for t, sc in enumerate(scores):
        p = jnp.exp(sc - mx)
        den = den + jnp.sum(p, axis=0, keepdims=True)
        acc = acc + jnp.sum(p * vbuf[slot, t], axis=0, keepdims=True)
    o_ref[0] = acc / den


def decode_attend(q, k_self, v_self, idx, cache_k, cache_v, page_table, slopes, *, past_len, name):
    n, _, hd = q.shape
    _, page, H, _ = cache_k.shape
    bsz, n_pages = page_table.shape
    n_slab = MOBA_TOPK * (MOBA_BLOCK // page)
    row = pl.BlockSpec((1, 1, hd), lambda s, idx_r, pt_r: (s, 0, 0))
    grid_spec = pltpu.PrefetchScalarGridSpec(
        num_scalar_prefetch=2, grid=(n,),
        in_specs=[row, row, row,
                  pl.BlockSpec((1, 1, LANES), lambda s, idx_r, pt_r: (s % H, 0, 0)),
                  pl.BlockSpec(memory_space=pl.ANY), pl.BlockSpec(memory_space=pl.ANY)],
        out_specs=row,
        scratch_shapes=[pltpu.VMEM((2, n_slab, page, hd), F32), pltpu.VMEM((2, n_slab, page, hd), F32),
                        pltpu.SemaphoreType.DMA((2, 2))])
    body = functools.partial(_decode_attend_body, n_heads=H, n_pages=n_pages, page=page, past_len=past_len)
    return pl.pallas_call(
        body, grid_spec=grid_spec, out_shape=jax.ShapeDtypeStruct((n, 1, hd), F32),
        compiler_params=_cparams(("arbitrary",)), name=name,
    )(idx, page_table.reshape(-1), q, k_self, v_self,
      jnp.broadcast_to(slopes.reshape(H, 1, 1), (H, 1, LANES)), cache_k, cache_v)


def kernel(x_prompt, x_sample, state_mlstm_C, state_mlstm_n, state_mlstm_m, state_conv, cache_k, cache_v, page_table, a_norm_g, a_w_in, a_b_if, a_head_g, a_w_out, kv_norm_g, w_kv, k_norm_g, b_norm_g, b_w_q, q_norm_g, b_w_o, f_norm_g, f_w_up, f_conv_w, f_conv_b, f_w_down):
    bsz, seq, d = x_prompt.shape
    dbsz, dseq, _ = x_sample.shape
    assert dseq == 1
    n_a, depth = a_w_in.shape[0], f_w_up.shape[0]
    n_b = depth - n_a
    dff = f_conv_w.shape[2]
    H = A_HEADS
    dk, dv = state_mlstm_C.shape[3], state_mlstm_C.shape[4]
    hd = d // B_HEADS
    past_len = page_table.shape[1] * cache_k.shape[1]
    s_main = 2 * H * dk + 2 * H * dv
    slopes = jnp.exp2(-8.0 * jnp.arange(1, B_HEADS + 1, dtype=F32) / B_HEADS)

    xp = x_prompt.reshape(bsz * seq, d)
    xs = x_sample.reshape(dbsz, d)
    Cp, Np, Mp, Cs, Ns, Ms, conv_p, conv_s = [], [], [], [], [], [], [], []

    dffp = -(-dff // FFN_TILE) * FFN_TILE
    pad = dffp - dff

    def conv_ffn(layer, xp, xs):
        w_u = jnp.pad(f_w_up[layer, :, :dff].astype(BF16), ((0, 0), (0, pad)))
        w_g = jnp.pad(f_w_up[layer, :, dff:].astype(BF16), ((0, 0), (0, pad)))
        w_down = jnp.pad(f_w_down[layer].astype(BF16), ((0, pad), (0, 0)))
        cw = jnp.pad(f_conv_w[layer], ((0, 0), (0, pad)))
        cb = jnp.pad(f_conv_b[layer], (0, pad))
        act_p, gt_p = ffn_up(xp, f_norm_g[layer], w_u, w_g, cw, cb, tile=FFN_TILE, seq_len=seq,
                             name=f"ffn_up_p{layer}")
        xp = matmul_res(act_p, w_down, xp, name=f"ffn_down_p{layer}")
        conv_p.append(gt_p.reshape(bsz, -1, SUBLANES, dffp)[:, -1, SUBLANES - (CONV_W - 1):, :dff])
        buf = state_conv[layer]
        act_s, g_s = ffn_up(xs, f_norm_g[layer], f_w_up, f_w_up, f_conv_w[layer], f_conv_b[layer], tile=LANES,
                            g_col0=dff // LANES, w_layer=layer, prev=(buf[:, 0, :], buf[:, 1, :]),
                            name=f"ffn_up_s{layer}")
        xs = matmul_res(act_s, f_w_down, xs, w_layer=layer, name=f"ffn_down_s{layer}")
        conv_s.append(jnp.stack([buf[:, 1, :], g_s], axis=1))
        return xp, xs

    for a in range(n_a):
        w_main = a_w_in[a][:, :s_main].astype(BF16)
        w_gate = jnp.zeros((d, LANES), BF16).at[:, :2 * H].set(a_w_in[a][:, s_main:].astype(BF16))
        w_out = a_w_out[a].astype(BF16)
        proj_p, gates_p = norm_matmul(xp, a_norm_g[a], w_main, w_side=w_gate, name=f"mlstm_in_p{a}")
        h_p, c1, n1, m1 = mlstm(
            proj_p.reshape(bsz, seq, s_main), gates_p.reshape(bsz, seq, LANES), a_b_if[a], a_head_g[a],
            jnp.zeros((bsz, H, dk, dv), F32), jnp.zeros((bsz, H, dk), F32), jnp.zeros((bsz, H), F32),
            n_valid=A_CHUNK, name=f"mlstm_p{a}")
        xp = matmul_res(h_p.reshape(bsz * seq, H * dv), w_out, xp, name=f"mlstm_out_p{a}")
        w_gate_s = jnp.zeros((d, LANES), F32).at[:, :2 * H].set(a_w_in[a][:, s_main:])
        proj_s, gates_s = norm_matmul(xs, a_norm_g[a], a_w_in, w_layer=a, n_cols=s_main, w_side=w_gate_s,
                                      name=f"mlstm_in_s{a}")
        proj_s = jnp.zeros((dbsz, A_CHUNK, s_main), F32).at[:, 0, :].set(proj_s)
        gates_s = jnp.zeros((dbsz, A_CHUNK, LANES), F32).at[:, 0, :].set(gates_s)
        h_s, c2, n2, m2 = mlstm(proj_s, gates_s, a_b_if[a], a_head_g[a],
                                state_mlstm_C[a], state_mlstm_n[a], state_mlstm_m[a],
                                n_valid=1, precise=True, name=f"mlstm_s{a}")
        xs = matmul_res(h_s[:, 0, :], a_w_out, xs, w_layer=a, name=f"mlstm_out_s{a}")
        Cp.append(c1), Np.append(n1), Mp.append(m1.reshape(bsz, H))
        Cs.append(c2), Ns.append(n2), Ms.append(m2.reshape(dbsz, H))
        xp, xs = conv_ffn(a, xp, xs)

    w_kv_b = w_kv.astype(BF16)
    kv_p, k_p4, v_p4 = norm_matmul(xp, kv_norm_g, w_kv_b, head_g=k_norm_g, n_norm_cols=d, heads_out=True,
                                   name="shared_kv_p")
    kv_s = norm_matmul(xs, kv_norm_g, w_kv, head_g=k_norm_g, n_norm_cols=d, name="shared_kv_s")
    k_s, v_s = kv_s[:, :d], kv_s[:, d:]
    kmean_s = paged_block_means(cache_k, page_table)
    kv_p3 = kv_p.reshape(bsz, seq, 2 * d)

    for bl in range(n_b):
        w_q = b_w_q[bl].astype(BF16)
        w_o = b_w_o[bl].astype(BF16)
        q_p = norm_matmul(xp, b_norm_g[bl], w_q, head_g=q_norm_g[bl], n_norm_cols=d,
                          head_scale=hd ** -0.5, name=f"moba_q_p{bl}")
        o_p = moba_seq(q_p.reshape(bsz, seq, d), kv_p3, slopes, name=f"moba_attn_p{bl}")
        xp = matmul_res(o_p.reshape(bsz * seq, d), w_o, xp, name=f"moba_out_p{bl}")
        q_s = norm_matmul(xs, b_norm_g[bl], b_w_q, w_layer=bl, head_g=q_norm_g[bl], n_norm_cols=d,
                          head_scale=hd ** -0.5, name=f"moba_q_s{bl}")
        idx = decode_select(q_s.reshape(dbsz, B_HEADS, hd), kmean_s)[:, :, :MOBA_TOPK].reshape(-1)
        o_s = decode_attend(q_s.reshape(dbsz * B_HEADS, 1, hd), k_s.reshape(dbsz * B_HEADS, 1, hd),
                            v_s.reshape(dbsz * B_HEADS, 1, hd), idx, cache_k, cache_v, page_table,
                            slopes, past_len=past_len, name=f"decode_attend{bl}")
        xs = matmul_res(o_s.reshape(dbsz, d), b_w_o, xs, w_layer=bl, name=f"moba_out_s{bl}")
        xp, xs = conv_ffn(n_a + bl, xp, xs)

    return (xp.reshape(bsz, seq, d), xs.reshape(dbsz, 1, d),
            jnp.stack(Cp), jnp.stack(Np), jnp.stack(Mp),
            jnp.stack(Cs), jnp.stack(Ns), jnp.stack(Ms),
            jnp.stack(conv_p), jnp.stack(conv_s),
            k_p4.reshape(bsz, seq, B_HEADS, hd), v_p4.reshape(bsz, seq, B_HEADS, hd),
            k_s.reshape(dbsz, 1, B_HEADS, hd), v_s.reshape(dbsz, 1, B_HEADS, hd))
```

```python
import functools
import math

import jax
import jax.numpy as jnp
from jax import lax
from jax.experimental import pallas as pl
from jax.experimental.pallas import tpu as pltpu

F32 = jnp.float32
BF16 = jnp.bfloat16

EPS = 1e-6
NEG = -1e30
LANES = 128
SUBLANES = 8
MXU_N = 256
VMEM_LIMIT = 56 * 1024 * 1024

A_HEADS = 8
A_CHUNK = 128
B_HEADS = 16
MOBA_BLOCK = 256
MOBA_TOPK = 3
MOBA_SEL_ROWS = 16
MOBA_HEADS_PER_STEP = 2
CONV_W = 3
FFN_TILE = 512


def _cparams(sem):
    return pltpu.CompilerParams(dimension_semantics=sem, vmem_limit_bytes=VMEM_LIMIT)


def _row_tile(m):
    return min(m, 1024)


def _col_tile(n, cap=1024):
    for t in (1024, 512, 256, 128):
        if t <= cap and n % t == 0:
            return t
    return n


def _split2(a):
    hi = a.astype(BF16)
    return hi, (a - hi.astype(F32)).astype(BF16)


def _mm(a, w):
    if w.dtype == BF16:
        return jnp.dot(a.astype(BF16), w, preferred_element_type=F32)
    (a_hi, a_lo), (w_hi, w_lo) = _split2(a), _split2(w)
    return (jnp.dot(a_hi, w_hi, preferred_element_type=F32) + jnp.dot(a_lo, w_hi, preferred_element_type=F32)
            + jnp.dot(a_hi, w_lo, preferred_element_type=F32))


def _bf16_round(x):
    return x.astype(BF16).astype(F32)


def _norm_matmul_body(*refs, n_norm_tiles, n_tiles, head_scale, has_side, heads_out):
    it = iter(refs)
    x_ref, g_ref, w_ref = next(it), next(it), next(it)
    hg_ref = next(it) if n_norm_tiles else None
    w2_ref = next(it) if has_side else None
    o_ref = next(it)
    o2_ref = next(it) if has_side else None
    ha_ref, hb_ref = (next(it), next(it)) if heads_out else (None, None)
    xn_ref = next(it)
    j = pl.program_id(1)

    @pl.when(j == 0)
    def _():
        x = x_ref[...]
        ms = jnp.mean(x * x, axis=-1, keepdims=True)
        xn = (x * lax.rsqrt(ms + EPS) * g_ref[...]).astype(xn_ref.dtype)
        xn_ref[...] = xn
        if has_side:
            o2_ref[...] = _mm(xn, w2_ref[...])

    acc = _mm(xn_ref[...], w_ref[...])

    def headnorm(a):
        parts = []
        for c in range(a.shape[1] // LANES):
            p = a[:, c * LANES:(c + 1) * LANES]
            ms = jnp.mean(p * p, axis=-1, keepdims=True)
            p = p * lax.rsqrt(ms + EPS) * hg_ref[...]
            if head_scale != 1.0:
                p = p * head_scale
            parts.append(p)
        return jnp.concatenate(parts, axis=1) if len(parts) > 1 else parts[0]

    def store(r, h_ref):
        o_ref[...] = r.astype(o_ref.dtype)
        if h_ref is not None:
            for h in range(r.shape[1] // LANES):
                h_ref[:, h, :] = r[:, h * LANES:(h + 1) * LANES]

    if n_norm_tiles == 0:
        store(acc, None)
    elif n_norm_tiles >= n_tiles:
        store(headnorm(acc), None)
    else:
        @pl.when(j < n_norm_tiles)
        def _():
            store(headnorm(acc), ha_ref)

        @pl.when(j >= n_norm_tiles)
        def _():
            store(acc, hb_ref)


def _weight_spec(w, layer, k, tn, col0=0):
    if w.ndim == 2:
        return pl.BlockSpec((k, tn), lambda i, j: (0, col0 + j))
    return pl.BlockSpec((None, k, tn), lambda i, j: (layer, 0, col0 + j))


def norm_matmul(x, g, w, *, w_layer=None, n_cols=None, head_g=None, n_norm_cols=0, head_scale=1.0,
                w_side=None, heads_out=False, name):
    m, k = x.shape
    n = w.shape[-1] if n_cols is None else n_cols
    tm, tn = _row_tile(m), _col_tile(n, cap=1024 if w.dtype == BF16 else 512)
    if heads_out:
        tm = min(tm, 512)
    n_tiles = n // tn
    assert n_norm_cols % tn == 0
    n_norm_tiles = n_norm_cols // tn
    assert not heads_out or (2 * n_norm_tiles == n_tiles and w_side is None and tn // LANES >= SUBLANES)
    in_specs = [pl.BlockSpec((tm, k), lambda i, j: (i, 0)),
                pl.BlockSpec((1, k), lambda i, j: (0, 0)),
                _weight_spec(w, w_layer, k, tn)]
    args = [x, g.reshape(1, k), w]
    if n_norm_tiles:
        in_specs.append(pl.BlockSpec((1, LANES), lambda i, j: (0, 0)))
        args.append(head_g.reshape(1, LANES))
    out_shape = [jax.ShapeDtypeStruct((m, n), F32)]
    out_specs = [pl.BlockSpec((tm, tn), lambda i, j: (i, j))]
    if w_side is not None:
        in_specs.append(pl.BlockSpec((k, LANES), lambda i, j: (0, 0)))
        args.append(w_side)
        out_shape.append(jax.ShapeDtypeStruct((m, LANES), F32))
        out_specs.append(pl.BlockSpec((tm, LANES), lambda i, j: (i, 0)))
    if heads_out:
        hpt = tn // LANES
        half = n_tiles // 2
        hshape = jax.ShapeDtypeStruct((m, half * hpt, LANES), F32)
        out_shape += [hshape, hshape]
        out_specs += [pl.BlockSpec((tm, hpt, LANES), lambda i, j: (i, jnp.minimum(j, half - 1), 0)),
                      pl.BlockSpec((tm, hpt, LANES), lambda i, j: (i, jnp.maximum(j - half, 0), 0))]
    body = functools.partial(_norm_matmul_body, n_norm_tiles=n_norm_tiles, n_tiles=n_tiles,
                             head_scale=head_scale, has_side=w_side is not None, heads_out=heads_out)
    outs = pl.pallas_call(
        body, grid=(m // tm, n_tiles), in_specs=in_specs, out_specs=out_specs, out_shape=out_shape,
        scratch_shapes=[pltpu.VMEM((tm, k), w.dtype)],
        compiler_params=_cparams(("parallel", "arbitrary")), name=name)(*args)
    return outs if len(outs) > 1 else outs[0]


def _matmul_res_body(a_ref, w_ref, r_ref, o_ref):
    o_ref[...] = r_ref[...] + _mm(a_ref[...], w_ref[...])


def matmul_res(a, w, res, *, w_layer=None, name):
    m, k = a.shape
    n = w.shape[-1]
    tm, tn = _row_tile(m), _col_tile(n, cap=512 if w.dtype == BF16 else 256)
    return pl.pallas_call(
        _matmul_res_body, grid=(m // tm, n // tn),
        in_specs=[pl.BlockSpec((tm, k), lambda i, j: (i, 0)),
                  _weight_spec(w, w_layer, k, tn),
                  pl.BlockSpec((tm, tn), lambda i, j: (i, j))],
        out_specs=pl.BlockSpec((tm, tn), lambda i, j: (i, j)),
        out_shape=jax.ShapeDtypeStruct((m, n), F32),
        compiler_params=_cparams(("parallel", "arbitrary")), name=name)(a, w, res)


def _ffn_up_body(*refs, seq_mode, blocks_per_seq):
    if seq_mode:
        x_ref, xh_ref, g_ref, wu_ref, wg_ref, cw_ref, cb_ref, act_ref, gt_ref, xn_ref, xhn_ref = refs
    else:
        x_ref, p2_ref, p1_ref, g_ref, wu_ref, wg_ref, cw_ref, cb_ref, act_ref, gt_ref, xn_ref = refs
    i, j = pl.program_id(0), pl.program_id(1)

    def norm(x):
        ms = jnp.mean(x * x, axis=-1, keepdims=True)
        return (x * lax.rsqrt(ms + EPS) * g_ref[...]).astype(xn_ref.dtype)

    @pl.when(j == 0)
    def _():
        xn_ref[...] = norm(x_ref[...])
        if seq_mode:
            xhn_ref[...] = norm(xh_ref[...])

    xn = xn_ref[...]
    tm = xn.shape[0]
    tw = wu_ref.shape[1]
    cw_ = min(tw, MXU_N)
    n_chunks = tw // cw_
    dots = {}

    def issue(c):
        cs = slice(c * cw_, (c + 1) * cw_)
        dots[c] = (_mm(xn, wu_ref[:, cs]), _mm(xn, wg_ref[:, cs]),
                   _mm(xhn_ref[...], wg_ref[:, cs]) if seq_mode else None)

    issue(0)
    for c in range(n_chunks):
        if c + 1 < n_chunks:
            issue(c + 1)
        cs = slice(c * cw_, (c + 1) * cw_)
        u, g, gh = dots.pop(c)
        if seq_mode:
            gh = jnp.where(i % blocks_per_seq == 0, 0.0, gh)
            prev1, prev2 = gh[SUBLANES - 1:SUBLANES, :], gh[SUBLANES - 2:SUBLANES - 1, :]
            row = lax.broadcasted_iota(jnp.int32, (SUBLANES, cw_), 0)
            r1, r2 = pltpu.roll(g, 1, 0), pltpu.roll(g, 2, 0)
            g1 = jnp.concatenate([jnp.where(row == 0, prev1, r1[:SUBLANES])] + ([r1[SUBLANES:]] if tm > SUBLANES else []), axis=0)
            g2 = jnp.concatenate([jnp.where(row == 0, prev2, jnp.where(row == 1, prev1, r2[:SUBLANES]))]
                                 + ([r2[SUBLANES:]] if tm > SUBLANES else []), axis=0)
            gt_ref[0, :, cs] = g[tm - SUBLANES:, :]
        else:
            g1, g2 = p1_ref[:, cs], p2_ref[:, cs]
            gt_ref[:, cs] = g
        gc = cb_ref[:, cs] + g2 * cw_ref[0:1, cs] + g1 * cw_ref[1:2, cs] + g * cw_ref[2:3, cs]
        act = gc * (1.0 / (1.0 + jnp.exp(-gc))) * u
        act_ref[:, cs] = act.astype(act_ref.dtype)


def ffn_up(x, g, w_u, w_g, conv_w, conv_b, *, tile, g_col0=0, w_layer=None, seq_len=None, prev=None, name):
    m, k = x.shape
    dff = conv_w.shape[1]
    tw = tile
    assert dff % tw == 0
    nt = dff // tw
    wdt = w_u.dtype
    tm = _row_tile(m)
    seq_mode = prev is None
    x_spec = pl.BlockSpec((tm, k), lambda i, j: (i, 0))
    common = [pl.BlockSpec((1, k), lambda i, j: (0, 0)),
              _weight_spec(w_u, w_layer, k, tw),
              _weight_spec(w_g, w_layer, k, tw, col0=g_col0),
              pl.BlockSpec((CONV_W, tw), lambda i, j: (0, j)),
              pl.BlockSpec((1, tw), lambda i, j: (0, j))]
    cargs = [g.reshape(1, k), w_u, w_g, conv_w, conv_b.reshape(1, dff)]
    act_spec = pl.BlockSpec((tm, tw), lambda i, j: (i, j))
    if seq_mode:
        tpb = tm // SUBLANES
        in_specs = [x_spec, pl.BlockSpec((SUBLANES, k), lambda i, j: (jnp.maximum(i * tpb - 1, 0), 0))] + common
        args = [x, x] + cargs
        out_shape = [jax.ShapeDtypeStruct((m, dff), wdt), jax.ShapeDtypeStruct((m // tm, SUBLANES, dff), F32)]
        out_specs = [act_spec, pl.BlockSpec((1, SUBLANES, tw), lambda i, j: (i, 0, j))]
        scratch = [pltpu.VMEM((tm, k), wdt), pltpu.VMEM((SUBLANES, k), wdt)]
        bps = seq_len // tm
    else:
        pspec = pl.BlockSpec((tm, tw), lambda i, j: (i, j))
        in_specs = [x_spec, pspec, pspec] + common
        args = [x, prev[0], prev[1]] + cargs
        out_shape = [jax.ShapeDtypeStruct((m, dff), wdt), jax.ShapeDtypeStruct((m, dff), F32)]
        out_specs = [act_spec, pl.BlockSpec((tm, tw), lambda i, j: (i, j))]
        scratch = [pltpu.VMEM((tm, k), wdt)]
        bps = 1
    body = functools.partial(_ffn_up_body, seq_mode=seq_mode, blocks_per_seq=bps)
    return pl.pallas_call(
        body, grid=(m // tm, nt), in_specs=in_specs, out_specs=out_specs, out_shape=out_shape,
        scratch_shapes=scratch, compiler_params=_cparams(("parallel", "arbitrary")), name=name)(*args)


def _log_sigmoid(x):
    return jnp.minimum(x, 0.0) - jnp.log(1.0 + jnp.exp(-jnp.abs(x)))


def _mlstm_body(proj_ref, gc_ref, gr_ref, bc_ref, br_ref, hg_ref, c0_ref, n0_ref, m0_ref,
                h_ref, c_ref, n_ref, m_ref, *, n_valid, dk, dv, precise):
    c_idx = pl.program_id(1)
    L = proj_ref.shape[1]
    H = A_HEADS

    @pl.when(c_idx == 0)
    def _():
        c_ref[...] = c0_ref[...]
        n_ref[...] = n0_ref[...]
        m_ref[...] = m0_ref[...]

    gates_c = gc_ref[0] + bc_ref[...]
    gates_r = gr_ref[0] + br_ref[...]
    row = lax.broadcasted_iota(jnp.int32, (L, L), 0)
    col = lax.broadcasted_iota(jnp.int32, (L, L), 1)
    tri = col <= row
    if n_valid < L:
        vc = lax.broadcasted_iota(jnp.int32, (L, 1), 0) < n_valid
        vr = lax.broadcasted_iota(jnp.int32, (1, L), 1) < n_valid

    s1, s2, s3 = H * dk, 2 * H * dk, 2 * H * dk + H * dv
    heads = range(H)

    def gate_stage(h):
        ig_c = gates_c[:, h:h + 1]
        lf_c = _log_sigmoid(gates_c[:, H + h:H + h + 1])
        ig_r = gates_r[h:h + 1, :]
        lf_r = _log_sigmoid(gates_r[H + h:H + h + 1, :])
        if n_valid < L:
            ig_c, lf_c = jnp.where(vc, ig_c, NEG), jnp.where(vc, lf_c, 0.0)
            ig_r, lf_r = jnp.where(vr, ig_r, NEG), jnp.where(vr, lf_r, 0.0)
        b_c = jnp.sum(jnp.where(tri, lf_r, 0.0), axis=1, keepdims=True)
        b_r = jnp.sum(jnp.where(row <= col, lf_c, 0.0), axis=0, keepdims=True)
        m_prev = m_ref[0, h:h + 1, :]
        logd = jnp.where(tri, b_c - b_r + ig_r, NEG)
        m_t = jnp.maximum(b_c + m_prev, jnp.max(logd, axis=1, keepdims=True))
        d = jnp.exp(logd - m_t)
        inter = jnp.exp(b_c + m_prev - m_t)
        m_new = m_t[L - 1:L, :]
        b_last = b_c[L - 1:L, :]
        w_s = jnp.exp(b_last - b_c + ig_c - m_new)
        decay = jnp.exp(b_last + m_prev - m_new)
        return dict(m_t=m_t, d=d, inter=inter, m_new=m_new, w_s=w_s, decay=decay)

    nn, nt, tn = (((1,), (0,)), ((), ())), (((1,), (1,)), ((), ())), (((0,), (0,)), ((), ()))

    def mm(a, b, dims):
        if not precise:
            return lax.dot_general(a.astype(BF16), b.astype(BF16), dims, preferred_element_type=F32)
        (a_hi, a_lo), (b_hi, b_lo) = _split2(a), _split2(b)
        return (lax.dot_general(a_hi, b_hi, dims, preferred_element_type=F32)
                + lax.dot_general(a_lo, b_hi, dims, preferred_element_type=F32)
                + lax.dot_general(a_hi, b_lo, dims, preferred_element_type=F32))

    rnd = (lambda x: x) if precise else _bf16_round

    def state_dots(h, gt):
        q = proj_ref[0, :, h * dk:(h + 1) * dk] * (dk ** -0.5)
        k = proj_ref[0, :, s1 + h * dk:s1 + (h + 1) * dk]
        v = proj_ref[0, :, s2 + h * dv:s2 + (h + 1) * dv]
        v = v if precise else v.astype(BF16)
        kw = k * gt["w_s"]
        return dict(qn=rnd(q), kn=rnd(k) * rnd(gt["w_s"]), v=v,
                    s=mm(q, k, nt), qc=mm(q, c_ref[0, h], nn), kv=mm(kw, v, tn))

    def value_dot(gt, st):
        w = st["s"] * gt["d"]
        return w, mm(w, st["v"], nn)

    def finish(h, gt, st, w, pv):
        n_prev = n_ref[0, h:h + 1, :]
        num = gt["inter"] * st["qc"] + pv
        den = gt["inter"] * jnp.sum(st["qn"] * rnd(n_prev), axis=1, keepdims=True) \
            + jnp.sum(w, axis=1, keepdims=True)
        hh = num / jnp.maximum(jnp.abs(den), jnp.exp(-gt["m_t"]))
        c_ref[0, h] = gt["decay"] * c_ref[0, h] + st["kv"]
        n_ref[0, h:h + 1, :] = gt["decay"] * n_prev + jnp.sum(st["kn"], axis=0, keepdims=True)
        m_ref[0, h:h + 1, :] = gt["m_new"]
        og = proj_ref[0, :, s3 + h * dv:s3 + (h + 1) * dv]
        hn = hh * lax.rsqrt(jnp.mean(hh * hh, axis=1, keepdims=True) + EPS)
        hn = hn * hg_ref[:, h * dv:(h + 1) * dv] * (1.0 / (1.0 + jnp.exp(-og)))
        h_ref[0, :, h * dv:(h + 1) * dv] = hn.astype(h_ref.dtype)

    gts = [gate_stage(h) for h in heads]
    sts = [state_dots(h, gts[h]) for h in heads]
    wps = [value_dot(gts[h], sts[h]) for h in heads]
    for h in heads:
        finish(h, gts[h], sts[h], *wps[h])


def mlstm(proj, gates, b_if, head_g, c0, n0, m0, *, n_valid, precise=False, name):
    bsz, t, _ = proj.shape
    H = A_HEADS
    dk, dv = c0.shape[2], c0.shape[3]
    L = min(A_CHUNK, t)
    nc = t // L
    gates_r = jnp.swapaxes(gates[:, :, :2 * H], 1, 2)
    bias_c = jnp.zeros((1, LANES), F32).at[0, :2 * H].set(b_if)
    bias_r = b_if.reshape(2 * H, 1)
    body = functools.partial(_mlstm_body, n_valid=n_valid, dk=dk, dv=dv, precise=precise)
    return pl.pallas_call(
        body, grid=(bsz, nc),
        in_specs=[pl.BlockSpec((1, L, proj.shape[2]), lambda b, c: (b, c, 0)),
                  pl.BlockSpec((1, L, LANES), lambda b, c: (b, c, 0)),
                  pl.BlockSpec((1, 2 * H, L), lambda b, c: (b, 0, c)),
                  pl.BlockSpec((1, LANES), lambda b, c: (0, 0)),
                  pl.BlockSpec((2 * H, 1), lambda b, c: (0, 0)),
                  pl.BlockSpec((1, H * dv), lambda b, c: (0, 0)),
                  pl.BlockSpec((1, H, dk, dv), lambda b, c: (b, 0, 0, 0)),
                  pl.BlockSpec((1, H, dk), lambda b, c: (b, 0, 0)),
                  pl.BlockSpec((1, H, 1), lambda b, c: (b, 0, 0))],
        out_specs=[pl.BlockSpec((1, L, H * dv), lambda b, c: (b, c, 0)),
                   pl.BlockSpec((1, H, dk, dv), lambda b, c: (b, 0, 0, 0)),
                   pl.BlockSpec((1, H, dk), lambda b, c: (b, 0, 0)),
                   pl.BlockSpec((1, H, 1), lambda b, c: (b, 0, 0))],
        out_shape=[jax.ShapeDtypeStruct((bsz, t, H * dv), F32 if precise else BF16),
                   jax.ShapeDtypeStruct((bsz, H, dk, dv), F32),
                   jax.ShapeDtypeStruct((bsz, H, dk), F32),
                   jax.ShapeDtypeStruct((bsz, H, 1), F32)],
        compiler_params=_cparams(("parallel", "arbitrary")), name=name,
    )(proj, gates, gates_r, bias_c, bias_r, head_g.reshape(1, H * dv), c0, n0, m0.reshape(bsz, H, 1))


def _moba_seq_body(q_ref, k_ref, v_ref, sl_ref, o_ref, kaug_ref, vt_ref, km_ref, mask_ref, *, nb, G):
    qi = pl.program_id(2)
    blk = MOBA_BLOCK
    hd = LANES

    @pl.when(qi == 0)
    def _():
        kk = lax.broadcasted_iota(jnp.int32, (blk, blk), 0)
        qq = lax.broadcasted_iota(jnp.int32, (blk, blk), 1)
        causal = jnp.where(kk <= qq, 0.0, NEG)
        mask_ref[0, :blk, :] = causal
        mask_ref[0, blk:, :] = jnp.full((blk, blk), NEG, F32)
        mask_ref[1, :blk, :] = jnp.zeros((blk, blk), F32)
        mask_ref[1, blk:, :] = causal
        lane = lax.broadcasted_iota(jnp.int32, (blk, hd), 1)
        row = lax.broadcasted_iota(jnp.int32, (blk, hd), 0)
        ones_row = jnp.where(lax.broadcasted_iota(jnp.int32, (MOBA_SEL_ROWS, blk), 0) == 0, 1.0, 0.0)
        km_ref[...] = jnp.zeros_like(km_ref)
        for g in range(G):
            slope2 = sl_ref[0, g:g + 1, :]
            for n in range(nb):
                j, r0 = n // 2, (n % 2) * blk
                kf = k_ref[0, n * blk:(n + 1) * blk, g * hd:(g + 1) * hd]
                a = (row + n * blk).astype(F32) * slope2
                a_hi = a.astype(BF16)
                a_r = a - a_hi.astype(F32)
                a_mid = a_r.astype(BF16)
                a_lo = (a_r - a_mid.astype(F32)).astype(BF16)
                aug = jnp.where(lane == n, 1.0, 0.0).astype(BF16)
                aug = jnp.where(lane == MOBA_SEL_ROWS, a_hi, aug)
                aug = jnp.where(lane == MOBA_SEL_ROWS + 1, a_mid, aug)
                aug = jnp.where(lane == MOBA_SEL_ROWS + 2, a_lo, aug)
                kaug_ref[g, j, r0:r0 + blk, :hd] = kf.astype(BF16)
                kaug_ref[g, j, r0:r0 + blk, hd:] = aug
                km_ref[g, n:n + 1, :] = jnp.mean(kf, axis=0, keepdims=True)
                vt = v_ref[0, n * blk:(n + 1) * blk, g * hd:(g + 1) * hd].T
                vt_ref[g, j, :hd, r0:r0 + blk] = vt.astype(BF16)
                vt_ref[g, j, hd:, r0:r0 + blk] = ones_row.astype(BF16)

    rowb = lax.broadcasted_iota(jnp.int32, (MOBA_SEL_ROWS, blk), 0)
    ones_rows = jnp.where(rowb < 3, 1.0, 0.0).astype(BF16)
    qaugs = []
    for g in range(G):
        qt = q_ref[0, :, g * hd:(g + 1) * hd].T
        q_hi = qt.astype(BF16)
        gate = jnp.dot(km_ref[g].astype(BF16), q_hi, preferred_element_type=F32)
        cnt = jnp.zeros(gate.shape, jnp.int32)
        for m in range(nb - 1):
            gm = gate[m:m + 1, :]
            beats = (gm > gate) | ((gm == gate) & (m < rowb))
            cnt = cnt + jnp.where(beats, jnp.where(m < qi, 1, 0), 0)
        keep = (rowb >= qi) | (cnt < MOBA_TOPK)
        selb = jnp.where(keep, 0.0, NEG).astype(BF16)
        qaugs.append(jnp.concatenate(
            [q_hi, selb, ones_rows, jnp.zeros((hd - 2 * MOBA_SEL_ROWS, blk), BF16)], axis=0))

    def pairs_step(js, carry, masks):
        ss = [[jnp.dot(kaug_ref[g, j], qaugs[g], preferred_element_type=F32) for g in range(G)]
              for j in js]
        m_run = [carry[g][0] for g in range(G)]
        acc = [carry[g][1] for g in range(G)]
        for t, j in enumerate(js):
            s_t = ss[t] if masks[t] is None else [s + masks[t] for s in ss[t]]
            m_new = [jnp.maximum(m_run[g], jnp.max(s_t[g], axis=0, keepdims=True)) for g in range(G)]
            ps = [jnp.exp(s_t[g] - m_new[g]).astype(BF16) for g in range(G)]
            pvs = [jnp.dot(vt_ref[g, j], ps[g], preferred_element_type=F32) for g in range(G)]
            acc = [jnp.exp(m_run[g] - m_new[g]) * acc[g] + pvs[g] for g in range(G)]
            m_run = m_new
        return tuple((m_run[g], acc[g]) for g in range(G))

    init = tuple((jnp.full((1, blk), -jnp.inf, F32), jnp.zeros((hd + MOBA_SEL_ROWS, blk), F32))
                 for _ in range(G))
    last = (qi + 2) // 2 - 1
    carry = lax.fori_loop(0, last // 2, lambda i, c: pairs_step([2 * i, 2 * i + 1], c, [None, None]), init)
    carry = lax.cond(
        last % 2 == 1,
        lambda c: pairs_step([last - 1, last], c, [None, mask_ref[qi % 2]]),
        lambda c: pairs_step([last], c, [mask_ref[qi % 2]]),
        carry)
    for g in range(G):
        _, acc = carry[g]
        o = (acc[:hd] / acc[hd:hd + 1]).T
        o_ref[0, :, g * hd:(g + 1) * hd] = o.astype(o_ref.dtype)


def moba_seq(q, kv, slopes, *, name):
    bsz, t, d = q.shape
    hd = LANES
    H = d // hd
    G = MOBA_HEADS_PER_STEP
    blk = MOBA_BLOCK
    nb = t // blk
    assert H % G == 0 and nb % 2 == 0 and nb <= MOBA_SEL_ROWS
    body = functools.partial(_moba_seq_body, nb=nb, G=G)
    return pl.pallas_call(
        body, grid=(bsz, H // G, nb),
        in_specs=[pl.BlockSpec((1, blk, G * hd), lambda b, h, i: (b, i, h)),
                  pl.BlockSpec((1, t, G * hd), lambda b, h, i: (b, 0, h)),
                  pl.BlockSpec((1, t, G * hd), lambda b, h, i: (b, 0, H // G + h)),
                  pl.BlockSpec((1, G, LANES), lambda b, h, i: (h, 0, 0))],
        out_specs=pl.BlockSpec((1, blk, G * hd), lambda b, h, i: (b, i, h)),
        out_shape=jax.ShapeDtypeStruct((bsz, t, d), BF16),
        scratch_shapes=[pltpu.VMEM((G, nb // 2, 2 * blk, 2 * hd), BF16),
                        pltpu.VMEM((G, nb // 2, hd + MOBA_SEL_ROWS, 2 * blk), BF16),
                        pltpu.VMEM((G, MOBA_SEL_ROWS, hd), F32),
                        pltpu.VMEM((2, 2 * blk, blk), F32)],
        compiler_params=_cparams(("parallel", "parallel", "arbitrary")), name=name,
    )(q, kv, kv, jnp.broadcast_to(slopes.reshape(H // G, G, 1), (H // G, G, LANES)))


def _page_mean_body(pt_ref, *refs, ppb):
    k_refs, o_ref = refs[:-1], refs[-1]
    for t in range(len(k_refs) // ppb):
        s = jnp.sum(k_refs[t * ppb][0], axis=0)
        for r in k_refs[t * ppb + 1:(t + 1) * ppb]:
            s = s + jnp.sum(r[0], axis=0)
        o_ref[0, t] = s * (1.0 / MOBA_BLOCK)


def paged_block_means(cache_k, page_table):
    _, page, H, hd = cache_k.shape
    bsz, n_pages = page_table.shape
    ppb = MOBA_BLOCK // page
    nblk = n_pages // ppb
    bps = 2 if nblk % 2 == 0 else 1
    pps = bps * ppb

    def page_spec(t):
        return pl.BlockSpec((1, page, H, hd), lambda b, n, pt: (pt[b, pps * n + t], 0, 0, 0))

    grid_spec = pltpu.PrefetchScalarGridSpec(
        num_scalar_prefetch=1, grid=(bsz, nblk // bps),
        in_specs=[page_spec(t) for t in range(pps)],
        out_specs=pl.BlockSpec((1, bps, H, hd), lambda b, n, pt: (b, n, 0, 0)))
    return pl.pallas_call(
        functools.partial(_page_mean_body, ppb=ppb), grid_spec=grid_spec,
        out_shape=jax.ShapeDtypeStruct((bsz, nblk, H, hd), F32),
        compiler_params=_cparams(("parallel", "arbitrary")), name="paged_block_means",
    )(page_table, *([cache_k] * pps))


def _decode_select_body(q_ref, km_ref, idx_ref, *, nblk):
    q = q_ref[0]
    lane = lax.broadcasted_iota(jnp.int32, (q.shape[0], LANES), 1)
    gate = jnp.full((q.shape[0], LANES), NEG, F32)
    for n in range(nblk):
        gn = jnp.sum(q * km_ref[0, n], axis=1, keepdims=True)
        gate = jnp.where(lane == n, gn, gate)
    out = jnp.zeros((q.shape[0], LANES), jnp.int32)
    for j in range(MOBA_TOPK):
        mx = jnp.max(gate, axis=1, keepdims=True)
        am = jnp.min(jnp.where(gate == mx, lane, LANES), axis=1, keepdims=True)
        out = jnp.where(lane == j, am, out)
        gate = jnp.where(lane == am, -jnp.inf, gate)
    idx_ref[0] = out


def decode_select(q, kmean):
    bsz, H, hd = q.shape
    nblk = kmean.shape[1]
    assert MOBA_TOPK <= nblk <= LANES
    return pl.pallas_call(
        functools.partial(_decode_select_body, nblk=nblk), grid=(bsz,),
        in_specs=[pl.BlockSpec((1, H, hd), lambda b: (b, 0, 0)),
                  pl.BlockSpec((1, nblk, H, hd), lambda b: (b, 0, 0, 0))],
        out_specs=pl.BlockSpec((1, H, LANES), lambda b: (b, 0, 0)),
        out_shape=jax.ShapeDtypeStruct((bsz, H, LANES), jnp.int32),
        compiler_params=_cparams(("parallel",)), name="decode_select")(q, kmean)


def _decode_attend_body(idx_ref, pt_ref, q_ref, ks_ref, vs_ref, sl_ref, k_hbm, v_hbm, o_ref,
                        kbuf, vbuf, sems, *, n_heads, n_pages, page, past_len):
    s = pl.program_id(0)
    n_steps = pl.num_programs(0)
    ppb = MOBA_BLOCK // page
    n_slab = MOBA_TOPK * ppb

    def slab_copies(step, slot):
        b, h = step // n_heads, step % n_heads
        cps = []
        for t in range(n_slab):
            blk_id = idx_ref[step * MOBA_TOPK + t // ppb]
            pg = pt_ref[b * n_pages + blk_id * ppb + t % ppb]
            cps.append(pltpu.make_async_copy(k_hbm.at[pg, :, h, :], kbuf.at[slot, t], sems.at[slot, 0]))
            cps.append(pltpu.make_async_copy(v_hbm.at[pg, :, h, :], vbuf.at[slot, t], sems.at[slot, 1]))
        return cps

    @pl.when(s == 0)
    def _():
        for cp in slab_copies(s, 0):
            cp.start()

    @pl.when(s + 1 < n_steps)
    def _():
        for cp in slab_copies(s + 1, (s + 1) % 2):
            cp.start()

    slot = s % 2
    for cp in slab_copies(s, slot):
        cp.wait()

    q = q_ref[0]
    slope = sl_ref[0][:, 0:1]
    sub = lax.broadcasted_iota(jnp.int32, (page, 1), 0)
    s_self = jnp.sum(q * ks_ref[0], axis=1, keepdims=True)
    scores = []
    for t in range(n_slab):
        blk_id = idx_ref[s * MOBA_TOPK + t // ppb]
        kpos = blk_id * MOBA_BLOCK + (t % ppb) * page + sub
        dist = (past_len - kpos).astype(F32)
        scores.append(jnp.sum(kbuf[slot, t] * q, axis=1, keepdims=True) - slope * dist)
    mx = s_self
    for sc in scores:
        mx = jnp.maximum(mx, jnp.max(sc, axis=0, keepdims=True))
    p_self = jnp.exp(s_self - mx)
    den = p_self
    acc = p_self * vs_ref[0]
    for t, sc in enumerate(scores):
        p = jnp.exp(sc - mx)
        den = den + jnp.sum(p, axis=0, keepdims=True)
        acc = acc + jnp.sum(p * vbuf[slot, t], axis=0, keepdims=True)
    o_ref[0] = acc / den


def decode_attend(q, k_self, v_self, idx, cache_k, cache_v, page_table, slopes, *, past_len, name):
    n, _, hd = q.shape
    _, page, H, _ = cache_k.shape
    bsz, n_pages = page_table.shape
    n_slab = MOBA_TOPK * (MOBA_BLOCK // page)
    row = pl.BlockSpec((1, 1, hd), lambda s, idx_r, pt_r: (s, 0, 0))
    grid_spec = pltpu.PrefetchScalarGridSpec(
        num_scalar_prefetch=2, grid=(n,),
        in_specs=[row, row, row,
                  pl.BlockSpec((1, 1, LANES), lambda s, idx_r, pt_r: (s % H, 0, 0)),
                  pl.BlockSpec(memory_space=pl.ANY), pl.BlockSpec(memory_space=pl.ANY)],
        out_specs=row,
        scratch_shapes=[pltpu.VMEM((2, n_slab, page, hd), F32), pltpu.VMEM((2, n_slab, page, hd), F32),
                        pltpu.SemaphoreType.DMA((2, 2))])
    body = functools.partial(_decode_attend_body, n_heads=H, n_pages=n_pages, page=page, past_len=past_len)
    return pl.pallas_call(
        body, grid_spec=grid_spec, out_shape=jax.ShapeDtypeStruct((n, 1, hd), F32),
        compiler_params=_cparams(("arbitrary",)), name=name,
    )(idx, page_table.reshape(-1), q, k_self, v_self,
      jnp.broadcast_to(slopes.reshape(H, 1, 1), (H, 1, LANES)), cache_k, cache_v)


def kernel(x_prompt, x_sample, state_mlstm_C, state_mlstm_n, state_mlstm_m, state_conv, cache_k, cache_v, page_table, a_norm_g, a_w_in, a_b_if, a_head_g, a_w_out, kv_norm_g, w_kv, k_norm_g, b_norm_g, b_w_q, q_norm_g, b_w_o, f_norm_g, f_w_up, f_conv_w, f_conv_b, f_w_down):
    bsz, seq, d = x_prompt.shape
    dbsz, dseq, _ = x_sample.shape
    assert dseq == 1
    n_a, depth = a_w_in.shape[0], f_w_up.shape[0]
    n_b = depth - n_a
    dff = f_conv_w.shape[2]
    H = A_HEADS
    dk, dv = state_mlstm_C.shape[3], state_mlstm_C.shape[4]
    hd = d // B_HEADS
    past_len = page_table.shape[1] * cache_k.shape[1]
    s_main = 2 * H * dk + 2 * H * dv
    slopes = jnp.exp2(-8.0 * jnp.arange(1, B_HEADS + 1, dtype=F32) / B_HEADS)

    xp = x_prompt.reshape(bsz * seq, d)
    xs = x_sample.reshape(dbsz, d)
    Cp, Np, Mp, Cs, Ns, Ms, conv_p, conv_s = [], [], [], [], [], [], [], []

    dffp = -(-dff // FFN_TILE) * FFN_TILE
    pad = dffp - dff

    zc = jnp.zeros((depth, d, pad), BF16)
    w_up_all = jnp.concatenate([f_w_up[:, :, :dff].astype(BF16), zc,
                                f_w_up[:, :, dff:].astype(BF16), zc], axis=2)
    w_down_all = jnp.concatenate([f_w_down.astype(BF16), jnp.zeros((depth, pad, d), BF16)], axis=1)
    w_main_all = a_w_in[:, :, :s_main].astype(BF16)

    def conv_ffn(layer, xp, xs):
        cw = jnp.pad(f_conv_w[layer], ((0, 0), (0, pad)))
        cb = jnp.pad(f_conv_b[layer], (0, pad))
        act_p, gt_p = ffn_up(xp, f_norm_g[layer], w_up_all, w_up_all, cw, cb, tile=FFN_TILE,
                             g_col0=dffp // FFN_TILE, w_layer=layer, seq_len=seq, name=f"ffn_up_p{layer}")
        xp = matmul_res(act_p, w_down_all, xp, w_layer=layer, name=f"ffn_down_p{layer}")
        conv_p.append(gt_p.reshape(bsz, -1, SUBLANES, dffp)[:, -1, SUBLANES - (CONV_W - 1):, :dff])
        buf = state_conv[layer]
        act_s, g_s = ffn_up(xs, f_norm_g[layer], f_w_up, f_w_up, f_conv_w[layer], f_conv_b[layer], tile=LANES,
                            g_col0=dff // LANES, w_layer=layer, prev=(buf[:, 0, :], buf[:, 1, :]),
                            name=f"ffn_up_s{layer}")
        xs = matmul_res(act_s, f_w_down, xs, w_layer=layer, name=f"ffn_down_s{layer}")
        conv_s.append(jnp.stack([buf[:, 1, :], g_s], axis=1))
        return xp, xs

    for a in range(n_a):
        w_gate = jnp.zeros((d, LANES), BF16).at[:, :2 * H].set(a_w_in[a][:, s_main:].astype(BF16))
        w_out = a_w_out[a].astype(BF16)
        proj_p, gates_p = norm_matmul(xp, a_norm_g[a], w_main_all, w_layer=a, w_side=w_gate,
                                      name=f"mlstm_in_p{a}")
        h_p, c1, n1, m1 = mlstm(
            proj_p.reshape(bsz, seq, s_main), gates_p.reshape(bsz, seq, LANES), a_b_if[a], a_head_g[a],
            jnp.zeros((bsz, H, dk, dv), F32), jnp.zeros((bsz, H, dk), F32), jnp.zeros((bsz, H), F32),
            n_valid=A_CHUNK, name=f"mlstm_p{a}")
        xp = matmul_res(h_p.reshape(bsz * seq, H * dv), w_out, xp, name=f"mlstm_out_p{a}")
        w_gate_s = jnp.zeros((d, LANES), F32).at[:, :2 * H].set(a_w_in[a][:, s_main:])
        proj_s, gates_s = norm_matmul(xs, a_norm_g[a], a_w_in[a, :, :s_main], w_side=w_gate_s,
                                      name=f"mlstm_in_s{a}")
        proj_s = jnp.zeros((dbsz, A_CHUNK, s_main), F32).at[:, 0, :].set(proj_s)
        gates_s = jnp.zeros((dbsz, A_CHUNK, LANES), F32).at[:, 0, :].set(gates_s)
        h_s, c2, n2, m2 = mlstm(proj_s, gates_s, a_b_if[a], a_head_g[a],
                                state_mlstm_C[a], state_mlstm_n[a], state_mlstm_m[a],
                                n_valid=1, precise=True, name=f"mlstm_s{a}")
        xs = matmul_res(h_s[:, 0, :], a_w_out, xs, w_layer=a, name=f"mlstm_out_s{a}")
        Cp.append(c1), Np.append(n1), Mp.append(m1.reshape(bsz, H))
        Cs.append(c2), Ns.append(n2), Ms.append(m2.reshape(dbsz, H))
        xp, xs = conv_ffn(a, xp, xs)

    w_kv_b = w_kv.astype(BF16)
    kv_p, k_p4, v_p4 = norm_matmul(xp, kv_norm_g, w_kv_b, head_g=k_norm_g, n_norm_cols=d, heads_out=True,
                                   name="shared_kv_p")
    kv_s = norm_matmul(xs, kv_norm_g, w_kv, head_g=k_norm_g, n_norm_cols=d, name="shared_kv_s")
    k_s, v_s = kv_s[:, :d], kv_s[:, d:]
    kmean_s = paged_block_means(cache_k, page_table)
    kv_p3 = kv_p.reshape(bsz, seq, 2 * d)

    for bl in range(n_b):
        w_q = b_w_q[bl].astype(BF16)
        w_o = b_w_o[bl].astype(BF16)
        q_p = norm_matmul(xp, b_norm_g[bl], w_q, head_g=q_norm_g[bl], n_norm_cols=d,
                          head_scale=hd ** -0.5, name=f"moba_q_p{bl}")
        o_p = moba_seq(q_p.reshape(bsz, seq, d), kv_p3, slopes, name=f"moba_attn_p{bl}")
        xp = matmul_res(o_p.reshape(bsz * seq, d), w_o, xp, name=f"moba_out_p{bl}")
        q_s = norm_matmul(xs, b_norm_g[bl], b_w_q, w_layer=bl, head_g=q_norm_g[bl], n_norm_cols=d,
                          head_scale=hd ** -0.5, name=f"moba_q_s{bl}")
        idx = decode_select(q_s.reshape(dbsz, B_HEADS, hd), kmean_s)[:, :, :MOBA_TOPK].reshape(-1)
        o_s = decode_attend(q_s.reshape(dbsz * B_HEADS, 1, hd), k_s.reshape(dbsz * B_HEADS, 1, hd),
                            v_s.reshape(dbsz * B_HEADS, 1, hd), idx, cache_k, cache_v, page_table,
                            slopes, past_len=past_len, name=f"decode_attend{bl}")
        xs = matmul_res(o_s.reshape(dbsz, d), b_w_o, xs, w_layer=bl, name=f"moba_out_s{bl}")
        xp, xs = conv_ffn(n_a + bl, xp, xs)

    return (xp.reshape(bsz, seq, d), xs.reshape(dbsz, 1, d),
            jnp.stack(Cp), jnp.stack(Np), jnp.stack(Mp),
            jnp.stack(Cs), jnp.stack(Ns), jnp.stack(Ms),
            jnp.stack(conv_p), jnp.stack(conv_s),
            k_p4.reshape(bsz, seq, B_HEADS, hd), v_p4.reshape(bsz, seq, B_HEADS, hd),
            k_s.reshape(dbsz, 1, B_HEADS, hd), v_s.reshape(dbsz, 1, B_HEADS, hd))
```

```python
import functools
import math

import jax
import jax.numpy as jnp
from jax import lax
from jax.experimental import pallas as pl
from jax.experimental.pallas import tpu as pltpu

F32 = jnp.float32
BF16 = jnp.bfloat16

EPS = 1e-6
NEG = -1e30
LANES = 128
SUBLANES = 8
MXU_N = 256
VMEM_LIMIT = 56 * 1024 * 1024

A_HEADS = 8
A_CHUNK = 128
B_HEADS = 16
MOBA_BLOCK = 256
MOBA_TOPK = 3
MOBA_SEL_ROWS = 16
MOBA_HEADS_PER_STEP = 2
CONV_W = 3
FFN_TILE = 512


def _cparams(sem):
    return pltpu.CompilerParams(dimension_semantics=sem, vmem_limit_bytes=VMEM_LIMIT)


def _row_tile(m):
    return min(m, 1024)


def _col_tile(n, cap=1024):
    for t in (1024, 512, 256, 128):
        if t <= cap and n % t == 0:
            return t
    return n


def _split2(a):
    hi = a.astype(BF16)
    return hi, (a - hi.astype(F32)).astype(BF16)


def _mm(a, w):
    if w.dtype == BF16:
        return jnp.dot(a.astype(BF16), w, preferred_element_type=F32)
    (a_hi, a_lo), (w_hi, w_lo) = _split2(a), _split2(w)
    return (jnp.dot(a_hi, w_hi, preferred_element_type=F32) + jnp.dot(a_lo, w_hi, preferred_element_type=F32)
            + jnp.dot(a_hi, w_lo, preferred_element_type=F32))


def _bf16_round(x):
    return x.astype(BF16).astype(F32)


def _norm_matmul_body(*refs, n_norm_tiles, n_tiles, head_scale, has_side, heads_out):
    it = iter(refs)
    x_ref, g_ref, w_ref = next(it), next(it), next(it)
    hg_ref = next(it) if n_norm_tiles else None
    w2_ref = next(it) if has_side else None
    o_ref = next(it)
    o2_ref = next(it) if has_side else None
    ha_ref, hb_ref = (next(it), next(it)) if heads_out else (None, None)
    xn_ref = next(it)
    j = pl.program_id(1)

    @pl.when(j == 0)
    def _():
        x = x_ref[...]
        ms = jnp.mean(x * x, axis=-1, keepdims=True)
        xn = (x * lax.rsqrt(ms + EPS) * g_ref[...]).astype(xn_ref.dtype)
        xn_ref[...] = xn
        if has_side:
            o2_ref[...] = _mm(xn, w2_ref[...])

    acc = _mm(xn_ref[...], w_ref[...])

    def headnorm(a):
        parts = []
        for c in range(a.shape[1] // LANES):
            p = a[:, c * LANES:(c + 1) * LANES]
            ms = jnp.mean(p * p, axis=-1, keepdims=True)
            p = p * lax.rsqrt(ms + EPS) * hg_ref[...]
            if head_scale != 1.0:
                p = p * head_scale
            parts.append(p)
        return jnp.concatenate(parts, axis=1) if len(parts) > 1 else parts[0]

    def store(r, h_ref):
        o_ref[...] = r.astype(o_ref.dtype)
        if h_ref is not None:
            for h in range(r.shape[1] // LANES):
                h_ref[:, h, :] = r[:, h * LANES:(h + 1) * LANES]

    if n_norm_tiles == 0:
        store(acc, None)
    elif n_norm_tiles >= n_tiles:
        store(headnorm(acc), None)
    else:
        @pl.when(j < n_norm_tiles)
        def _():
            store(headnorm(acc), ha_ref)

        @pl.when(j >= n_norm_tiles)
        def _():
            store(acc, hb_ref)


def _weight_spec(w, layer, k, tn, col0=0):
    if w.ndim == 2:
        return pl.BlockSpec((k, tn), lambda i, j: (0, col0 + j))
    return pl.BlockSpec((None, k, tn), lambda i, j: (layer, 0, col0 + j))


def norm_matmul(x, g, w, *, w_layer=None, n_cols=None, head_g=None, n_norm_cols=0, head_scale=1.0,
                w_side=None, heads_out=False, name):
    m, k = x.shape
    n = w.shape[-1] if n_cols is None else n_cols
    tm, tn = _row_tile(m), _col_tile(n, cap=1024 if w.dtype == BF16 else 512)
    if heads_out:
        tm = min(tm, 512)
    n_tiles = n // tn
    assert n_norm_cols % tn == 0
    n_norm_tiles = n_norm_cols // tn
    assert not heads_out or (2 * n_norm_tiles == n_tiles and w_side is None and tn // LANES >= SUBLANES)
    in_specs = [pl.BlockSpec((tm, k), lambda i, j: (i, 0)),
                pl.BlockSpec((1, k), lambda i, j: (0, 0)),
                _weight_spec(w, w_layer, k, tn)]
    args = [x, g.reshape(1, k), w]
    if n_norm_tiles:
        in_specs.append(pl.BlockSpec((1, LANES), lambda i, j: (0, 0)))
        args.append(head_g.reshape(1, LANES))
    out_shape = [jax.ShapeDtypeStruct((m, n), F32)]
    out_specs = [pl.BlockSpec((tm, tn), lambda i, j: (i, j))]
    if w_side is not None:
        in_specs.append(pl.BlockSpec((k, LANES), lambda i, j: (0, 0)))
        args.append(w_side)
        out_shape.append(jax.ShapeDtypeStruct((m, LANES), F32))
        out_specs.append(pl.BlockSpec((tm, LANES), lambda i, j: (i, 0)))
    if heads_out:
        hpt = tn // LANES
        half = n_tiles // 2
        hshape = jax.ShapeDtypeStruct((m, half * hpt, LANES), F32)
        out_shape += [hshape, hshape]
        out_specs += [pl.BlockSpec((tm, hpt, LANES), lambda i, j: (i, jnp.minimum(j, half - 1), 0)),
                      pl.BlockSpec((tm, hpt, LANES), lambda i, j: (i, jnp.maximum(j - half, 0), 0))]
    body = functools.partial(_norm_matmul_body, n_norm_tiles=n_norm_tiles, n_tiles=n_tiles,
                             head_scale=head_scale, has_side=w_side is not None, heads_out=heads_out)
    outs = pl.pallas_call(
        body, grid=(m // tm, n_tiles), in_specs=in_specs, out_specs=out_specs, out_shape=out_shape,
        scratch_shapes=[pltpu.VMEM((tm, k), w.dtype)],
        compiler_params=_cparams(("parallel", "arbitrary")), name=name)(*args)
    return outs if len(outs) > 1 else outs[0]


def _matmul_res_body(a_ref, w_ref, r_ref, o_ref):
    o_ref[...] = r_ref[...] + _mm(a_ref[...], w_ref[...])


def matmul_res(a, w, res, *, w_layer=None, name):
    m, k = a.shape
    n = w.shape[-1]
    cap = 256 if w.dtype != BF16 else (1024 if k <= 2048 else 512)
    tm, tn = _row_tile(m), _col_tile(n, cap=cap)
    return pl.pallas_call(
        _matmul_res_body, grid=(m // tm, n // tn),
        in_specs=[pl.BlockSpec((tm, k), lambda i, j: (i, 0)),
                  _weight_spec(w, w_layer, k, tn),
                  pl.BlockSpec((tm, tn), lambda i, j: (i, j))],
        out_specs=pl.BlockSpec((tm, tn), lambda i, j: (i, j)),
        out_shape=jax.ShapeDtypeStruct((m, n), F32),
        compiler_params=_cparams(("parallel", "arbitrary")), name=name)(a, w, res)


def _ffn_up_body(*refs, seq_mode, blocks_per_seq):
    if seq_mode:
        x_ref, xh_ref, g_ref, wu_ref, wg_ref, cw_ref, cb_ref, act_ref, gt_ref, xn_ref, xhn_ref = refs
    else:
        x_ref, p2_ref, p1_ref, g_ref, wu_ref, wg_ref, cw_ref, cb_ref, act_ref, gt_ref, xn_ref = refs
    i, j = pl.program_id(0), pl.program_id(1)

    def norm(x):
        ms = jnp.mean(x * x, axis=-1, keepdims=True)
        return (x * lax.rsqrt(ms + EPS) * g_ref[...]).astype(xn_ref.dtype)

    @pl.when(j == 0)
    def _():
        xn_ref[...] = norm(x_ref[...])
        if seq_mode:
            xhn_ref[...] = norm(xh_ref[...])

    xn = xn_ref[...]
    tm = xn.shape[0]
    tw = wu_ref.shape[1]
    cw_ = min(tw, MXU_N)
    for c in range(tw // cw_):
        cs = slice(c * cw_, (c + 1) * cw_)
        u = _mm(xn, wu_ref[:, cs])
        g = _mm(xn, wg_ref[:, cs])
        if seq_mode:
            gh = _mm(xhn_ref[...], wg_ref[:, cs])
            gh = jnp.where(i % blocks_per_seq == 0, 0.0, gh)
            prev1, prev2 = gh[SUBLANES - 1:SUBLANES, :], gh[SUBLANES - 2:SUBLANES - 1, :]
            row = lax.broadcasted_iota(jnp.int32, (SUBLANES, cw_), 0)
            r1, r2 = pltpu.roll(g, 1, 0), pltpu.roll(g, 2, 0)
            g1 = jnp.concatenate([jnp.where(row == 0, prev1, r1[:SUBLANES])] + ([r1[SUBLANES:]] if tm > SUBLANES else []), axis=0)
            g2 = jnp.concatenate([jnp.where(row == 0, prev2, jnp.where(row == 1, prev1, r2[:SUBLANES]))]
                                 + ([r2[SUBLANES:]] if tm > SUBLANES else []), axis=0)
            gt_ref[0, :, cs] = g[tm - SUBLANES:, :]
        else:
            g1, g2 = p1_ref[:, cs], p2_ref[:, cs]
            gt_ref[:, cs] = g
        gc = cb_ref[:, cs] + g2 * cw_ref[0:1, cs] + g1 * cw_ref[1:2, cs] + g * cw_ref[2:3, cs]
        act = gc * (1.0 / (1.0 + jnp.exp(-gc))) * u
        act_ref[:, cs] = act.astype(act_ref.dtype)


def ffn_up(x, g, w_u, w_g, conv_w, conv_b, *, tile, g_col0=0, w_layer=None, seq_len=None, prev=None, name):
    m, k = x.shape
    dff = conv_w.shape[1]
    tw = tile
    assert dff % tw == 0
    nt = dff // tw
    wdt = w_u.dtype
    tm = _row_tile(m)
    seq_mode = prev is None
    x_spec = pl.BlockSpec((tm, k), lambda i, j: (i, 0))
    common = [pl.BlockSpec((1, k), lambda i, j: (0, 0)),
              _weight_spec(w_u, w_layer, k, tw),
              _weight_spec(w_g, w_layer, k, tw, col0=g_col0),
              pl.BlockSpec((CONV_W, tw), lambda i, j: (0, j)),
              pl.BlockSpec((1, tw), lambda i, j: (0, j))]
    cargs = [g.reshape(1, k), w_u, w_g, conv_w, conv_b.reshape(1, dff)]
    act_spec = pl.BlockSpec((tm, tw), lambda i, j: (i, j))
    if seq_mode:
        tpb = tm // SUBLANES
        in_specs = [x_spec, pl.BlockSpec((SUBLANES, k), lambda i, j: (jnp.maximum(i * tpb - 1, 0), 0))] + common
        args = [x, x] + cargs
        out_shape = [jax.ShapeDtypeStruct((m, dff), wdt), jax.ShapeDtypeStruct((m // tm, SUBLANES, dff), F32)]
        out_specs = [act_spec, pl.BlockSpec((1, SUBLANES, tw), lambda i, j: (i, 0, j))]
        scratch = [pltpu.VMEM((tm, k), wdt), pltpu.VMEM((SUBLANES, k), wdt)]
        bps = seq_len // tm
    else:
        pspec = pl.BlockSpec((tm, tw), lambda i, j: (i, j))
        in_specs = [x_spec, pspec, pspec] + common
        args = [x, prev[0], prev[1]] + cargs
        out_shape = [jax.ShapeDtypeStruct((m, dff), wdt), jax.ShapeDtypeStruct((m, dff), F32)]
        out_specs = [act_spec, pl.BlockSpec((tm, tw), lambda i, j: (i, j))]
        scratch = [pltpu.VMEM((tm, k), wdt)]
        bps = 1
    body = functools.partial(_ffn_up_body, seq_mode=seq_mode, blocks_per_seq=bps)
    return pl.pallas_call(
        body, grid=(m // tm, nt), in_specs=in_specs, out_specs=out_specs, out_shape=out_shape,
        scratch_shapes=scratch, compiler_params=_cparams(("parallel", "arbitrary")), name=name)(*args)


def _log_sigmoid(x):
    return jnp.minimum(x, 0.0) - jnp.log(1.0 + jnp.exp(-jnp.abs(x)))


def _mlstm_body(proj_ref, gc_ref, gr_ref, bc_ref, br_ref, hg_ref, c0_ref, n0_ref, m0_ref,
                h_ref, c_ref, n_ref, m_ref, *, n_valid, dk, dv, precise):
    c_idx = pl.program_id(1)
    L = proj_ref.shape[1]
    H = A_HEADS

    @pl.when(c_idx == 0)
    def _():
        c_ref[...] = c0_ref[...]
        n_ref[...] = n0_ref[...]
        m_ref[...] = m0_ref[...]

    gates_c = gc_ref[0] + bc_ref[...]
    gates_r = gr_ref[0] + br_ref[...]
    row = lax.broadcasted_iota(jnp.int32, (L, L), 0)
    col = lax.broadcasted_iota(jnp.int32, (L, L), 1)
    tri = col <= row
    if n_valid < L:
        vc = lax.broadcasted_iota(jnp.int32, (L, 1), 0) < n_valid
        vr = lax.broadcasted_iota(jnp.int32, (1, L), 1) < n_valid

    s1, s2, s3 = H * dk, 2 * H * dk, 2 * H * dk + H * dv
    heads = range(H)

    def gate_stage(h):
        ig_c = gates_c[:, h:h + 1]
        lf_c = _log_sigmoid(gates_c[:, H + h:H + h + 1])
        ig_r = gates_r[h:h + 1, :]
        lf_r = _log_sigmoid(gates_r[H + h:H + h + 1, :])
        if n_valid < L:
            ig_c, lf_c = jnp.where(vc, ig_c, NEG), jnp.where(vc, lf_c, 0.0)
            ig_r, lf_r = jnp.where(vr, ig_r, NEG), jnp.where(vr, lf_r, 0.0)
        b_c = jnp.sum(jnp.where(tri, lf_r, 0.0), axis=1, keepdims=True)
        b_r = jnp.sum(jnp.where(row <= col, lf_c, 0.0), axis=0, keepdims=True)
        m_prev = m_ref[0, h:h + 1, :]
        logd = jnp.where(tri, b_c - b_r + ig_r, NEG)
        m_t = jnp.maximum(b_c + m_prev, jnp.max(logd, axis=1, keepdims=True))
        d = jnp.exp(logd - m_t)
        inter = jnp.exp(b_c + m_prev - m_t)
        m_new = m_t[L - 1:L, :]
        b_last = b_c[L - 1:L, :]
        w_s = jnp.exp(b_last - b_c + ig_c - m_new)
        decay = jnp.exp(b_last + m_prev - m_new)
        return dict(m_t=m_t, d=d, inter=inter, m_new=m_new, w_s=w_s, decay=decay)

    nn, nt, tn = (((1,), (0,)), ((), ())), (((1,), (1,)), ((), ())), (((0,), (0,)), ((), ()))

    def mm(a, b, dims):
        if not precise:
            return lax.dot_general(a.astype(BF16), b.astype(BF16), dims, preferred_element_type=F32)
        (a_hi, a_lo), (b_hi, b_lo) = _split2(a), _split2(b)
        return (lax.dot_general(a_hi, b_hi, dims, preferred_element_type=F32)
                + lax.dot_general(a_lo, b_hi, dims, preferred_element_type=F32)
                + lax.dot_general(a_hi, b_lo, dims, preferred_element_type=F32))

    rnd = (lambda x: x) if precise else _bf16_round

    def state_dots(h, gt):
        q = proj_ref[0, :, h * dk:(h + 1) * dk] * (dk ** -0.5)
        k = proj_ref[0, :, s1 + h * dk:s1 + (h + 1) * dk]
        v = proj_ref[0, :, s2 + h * dv:s2 + (h + 1) * dv]
        v = v if precise else v.astype(BF16)
        kw = k * gt["w_s"]
        return dict(qn=rnd(q), kn=rnd(k) * rnd(gt["w_s"]), v=v,
                    s=mm(q, k, nt), qc=mm(q, c_ref[0, h], nn), kv=mm(kw, v, tn))

    def value_dot(gt, st):
        w = st["s"] * gt["d"]
        return w, mm(w, st["v"], nn)

    def finish(h, gt, st, w, pv):
        n_prev = n_ref[0, h:h + 1, :]
        num = gt["inter"] * st["qc"] + pv
        den = gt["inter"] * jnp.sum(st["qn"] * rnd(n_prev), axis=1, keepdims=True) \
            + jnp.sum(w, axis=1, keepdims=True)
        hh = num / jnp.maximum(jnp.abs(den), jnp.exp(-gt["m_t"]))
        c_ref[0, h] = gt["decay"] * c_ref[0, h] + st["kv"]
        n_ref[0, h:h + 1, :] = gt["decay"] * n_prev + jnp.sum(st["kn"], axis=0, keepdims=True)
        m_ref[0, h:h + 1, :] = gt["m_new"]
        og = proj_ref[0, :, s3 + h * dv:s3 + (h + 1) * dv]
        hn = hh * lax.rsqrt(jnp.mean(hh * hh, axis=1, keepdims=True) + EPS)
        hn = hn * hg_ref[:, h * dv:(h + 1) * dv] * (1.0 / (1.0 + jnp.exp(-og)))
        h_ref[0, :, h * dv:(h + 1) * dv] = hn.astype(h_ref.dtype)

    gts = [gate_stage(h) for h in heads]
    sts = [state_dots(h, gts[h]) for h in heads]
    wps = [value_dot(gts[h], sts[h]) for h in heads]
    for h in heads:
        finish(h, gts[h], sts[h], *wps[h])


def mlstm(proj, gates, b_if, head_g, c0, n0, m0, *, n_valid, precise=False, name):
    bsz, t, _ = proj.shape
    H = A_HEADS
    dk, dv = c0.shape[2], c0.shape[3]
    L = min(A_CHUNK, t)
    nc = t // L
    gates_r = jnp.swapaxes(gates[:, :, :2 * H], 1, 2)
    bias_c = jnp.zeros((1, LANES), F32).at[0, :2 * H].set(b_if)
    bias_r = b_if.reshape(2 * H, 1)
    body = functools.partial(_mlstm_body, n_valid=n_valid, dk=dk, dv=dv, precise=precise)
    return pl.pallas_call(
        body, grid=(bsz, nc),
        in_specs=[pl.BlockSpec((1, L, proj.shape[2]), lambda b, c: (b, c, 0)),
                  pl.BlockSpec((1, L, LANES), lambda b, c: (b, c, 0)),
                  pl.BlockSpec((1, 2 * H, L), lambda b, c: (b, 0, c)),
                  pl.BlockSpec((1, LANES), lambda b, c: (0, 0)),
                  pl.BlockSpec((2 * H, 1), lambda b, c: (0, 0)),
                  pl.BlockSpec((1, H * dv), lambda b, c: (0, 0)),
                  pl.BlockSpec((1, H, dk, dv), lambda b, c: (b, 0, 0, 0)),
                  pl.BlockSpec((1, H, dk), lambda b, c: (b, 0, 0)),
                  pl.BlockSpec((1, H, 1), lambda b, c: (b, 0, 0))],
        out_specs=[pl.BlockSpec((1, L, H * dv), lambda b, c: (b, c, 0)),
                   pl.BlockSpec((1, H, dk, dv), lambda b, c: (b, 0, 0, 0)),
                   pl.BlockSpec((1, H, dk), lambda b, c: (b, 0, 0)),
                   pl.BlockSpec((1, H, 1), lambda b, c: (b, 0, 0))],
        out_shape=[jax.ShapeDtypeStruct((bsz, t, H * dv), F32 if precise else BF16),
                   jax.ShapeDtypeStruct((bsz, H, dk, dv), F32),
                   jax.ShapeDtypeStruct((bsz, H, dk), F32),
                   jax.ShapeDtypeStruct((bsz, H, 1), F32)],
        compiler_params=_cparams(("parallel", "arbitrary")), name=name,
    )(proj, gates, gates_r, bias_c, bias_r, head_g.reshape(1, H * dv), c0, n0, m0.reshape(bsz, H, 1))


def _moba_seq_body(q_ref, k_ref, v_ref, sl_ref, o_ref, kaug_ref, vt_ref, km_ref, mask_ref, *, nb, G):
    qi = pl.program_id(2)
    blk = MOBA_BLOCK
    hd = LANES

    @pl.when(qi == 0)
    def _():
        kk = lax.broadcasted_iota(jnp.int32, (blk, blk), 0)
        qq = lax.broadcasted_iota(jnp.int32, (blk, blk), 1)
        causal = jnp.where(kk <= qq, 0.0, NEG)
        mask_ref[0, :blk, :] = causal
        mask_ref[0, blk:, :] = jnp.full((blk, blk), NEG, F32)
        mask_ref[1, :blk, :] = jnp.zeros((blk, blk), F32)
        mask_ref[1, blk:, :] = causal
        lane = lax.broadcasted_iota(jnp.int32, (blk, hd), 1)
        row = lax.broadcasted_iota(jnp.int32, (blk, hd), 0)
        ones_row = jnp.where(lax.broadcasted_iota(jnp.int32, (MOBA_SEL_ROWS, blk), 0) == 0, 1.0, 0.0)
        km_ref[...] = jnp.zeros_like(km_ref)
        for g in range(G):
            slope2 = sl_ref[0, g:g + 1, :]
            for n in range(nb):
                j, r0 = n // 2, (n % 2) * blk
                kf = k_ref[0, n * blk:(n + 1) * blk, g * hd:(g + 1) * hd]
                a = (row + n * blk).astype(F32) * slope2
                a_hi = a.astype(BF16)
                a_r = a - a_hi.astype(F32)
                a_mid = a_r.astype(BF16)
                a_lo = (a_r - a_mid.astype(F32)).astype(BF16)
                aug = jnp.where(lane == n, 1.0, 0.0).astype(BF16)
                aug = jnp.where(lane == MOBA_SEL_ROWS, a_hi, aug)
                aug = jnp.where(lane == MOBA_SEL_ROWS + 1, a_mid, aug)
                aug = jnp.where(lane == MOBA_SEL_ROWS + 2, a_lo, aug)
                kaug_ref[g, j, r0:r0 + blk, :hd] = kf.astype(BF16)
                kaug_ref[g, j, r0:r0 + blk, hd:] = aug
                km_ref[g, n:n + 1, :] = jnp.mean(kf, axis=0, keepdims=True)
                vt = v_ref[0, n * blk:(n + 1) * blk, g * hd:(g + 1) * hd].T
                vt_ref[g, j, :hd, r0:r0 + blk] = vt.astype(BF16)
                vt_ref[g, j, hd:, r0:r0 + blk] = ones_row.astype(BF16)

    rowb = lax.broadcasted_iota(jnp.int32, (MOBA_SEL_ROWS, blk), 0)
    ones_rows = jnp.where(rowb < 3, 1.0, 0.0).astype(BF16)
    qaugs = []
    for g in range(G):
        qt = q_ref[0, :, g * hd:(g + 1) * hd].T
        q_hi = qt.astype(BF16)
        gate = jnp.dot(km_ref[g].astype(BF16), q_hi, preferred_element_type=F32)
        cnt = jnp.zeros(gate.shape, jnp.int32)
        for m in range(nb - 1):
            gm = gate[m:m + 1, :]
            beats = (gm > gate) | ((gm == gate) & (m < rowb))
            cnt = cnt + jnp.where(beats, jnp.where(m < qi, 1, 0), 0)
        keep = (rowb >= qi) | (cnt < MOBA_TOPK)
        selb = jnp.where(keep, 0.0, NEG).astype(BF16)
        qaugs.append(jnp.concatenate(
            [q_hi, selb, ones_rows, jnp.zeros((hd - 2 * MOBA_SEL_ROWS, blk), BF16)], axis=0))

    def pairs_step(js, carry, masks):
        ss = [[jnp.dot(kaug_ref[g, j], qaugs[g], preferred_element_type=F32) for g in range(G)]
              for j in js]
        m_run = [carry[g][0] for g in range(G)]
        acc = [carry[g][1] for g in range(G)]
        for t, j in enumerate(js):
            s_t = ss[t] if masks[t] is None else [s + masks[t] for s in ss[t]]
            m_new = [jnp.maximum(m_run[g], jnp.max(s_t[g], axis=0, keepdims=True)) for g in range(G)]
            ps = [jnp.exp(s_t[g] - m_new[g]).astype(BF16) for g in range(G)]
            pvs = [jnp.dot(vt_ref[g, j], ps[g], preferred_element_type=F32) for g in range(G)]
            acc = [jnp.exp(m_run[g] - m_new[g]) * acc[g] + pvs[g] for g in range(G)]
            m_run = m_new
        return tuple((m_run[g], acc[g]) for g in range(G))

    init = tuple((jnp.full((1, blk), -jnp.inf, F32), jnp.zeros((hd + MOBA_SEL_ROWS, blk), F32))
                 for _ in range(G))
    last = (qi + 2) // 2 - 1
    carry = lax.fori_loop(0, last // 2, lambda i, c: pairs_step([2 * i, 2 * i + 1], c, [None, None]), init)
    carry = lax.cond(
        last % 2 == 1,
        lambda c: pairs_step([last - 1, last], c, [None, mask_ref[qi % 2]]),
        lambda c: pairs_step([last], c, [mask_ref[qi % 2]]),
        carry)
    for g in range(G):
        _, acc = carry[g]
        o = (acc[:hd] / acc[hd:hd + 1]).T
        o_ref[0, :, g * hd:(g + 1) * hd] = o.astype(o_ref.dtype)


def moba_seq(q, kv, slopes, *, name):
    bsz, t, d = q.shape
    hd = LANES
    H = d // hd
    G = MOBA_HEADS_PER_STEP
    blk = MOBA_BLOCK
    nb = t // blk
    assert H % G == 0 and nb % 2 == 0 and nb <= MOBA_SEL_ROWS
    body = functools.partial(_moba_seq_body, nb=nb, G=G)
    return pl.pallas_call(
        body, grid=(bsz, H // G, nb),
        in_specs=[pl.BlockSpec((1, blk, G * hd), lambda b, h, i: (b, i, h)),
                  pl.BlockSpec((1, t, G * hd), lambda b, h, i: (b, 0, h)),
                  pl.BlockSpec((1, t, G * hd), lambda b, h, i: (b, 0, H // G + h)),
                  pl.BlockSpec((1, G, LANES), lambda b, h, i: (h, 0, 0))],
        out_specs=pl.BlockSpec((1, blk, G * hd), lambda b, h, i: (b, i, h)),
        out_shape=jax.ShapeDtypeStruct((bsz, t, d), BF16),
        scratch_shapes=[pltpu.VMEM((G, nb // 2, 2 * blk, 2 * hd), BF16),
                        pltpu.VMEM((G, nb // 2, hd + MOBA_SEL_ROWS, 2 * blk), BF16),
                        pltpu.VMEM((G, MOBA_SEL_ROWS, hd), F32),
                        pltpu.VMEM((2, 2 * blk, blk), F32)],
        compiler_params=_cparams(("parallel", "parallel", "arbitrary")), name=name,
    )(q, kv, kv, jnp.broadcast_to(slopes.reshape(H // G, G, 1), (H // G, G, LANES)))


def _page_mean_body(pt_ref, *refs, ppb):
    k_refs, o_ref = refs[:-1], refs[-1]
    for t in range(len(k_refs) // ppb):
        s = jnp.sum(k_refs[t * ppb][0], axis=0)
        for r in k_refs[t * ppb + 1:(t + 1) * ppb]:
            s = s + jnp.sum(r[0], axis=0)
        o_ref[0, t] = s * (1.0 / MOBA_BLOCK)


def paged_block_means(cache_k, page_table):
    _, page, H, hd = cache_k.shape
    bsz, n_pages = page_table.shape
    ppb = MOBA_BLOCK // page
    nblk = n_pages // ppb
    bps = 2 if nblk % 2 == 0 else 1
    pps = bps * ppb

    def page_spec(t):
        return pl.BlockSpec((1, page, H, hd), lambda b, n, pt: (pt[b, pps * n + t], 0, 0, 0))

    grid_spec = pltpu.PrefetchScalarGridSpec(
        num_scalar_prefetch=1, grid=(bsz, nblk // bps),
        in_specs=[page_spec(t) for t in range(pps)],
        out_specs=pl.BlockSpec((1, bps, H, hd), lambda b, n, pt: (b, n, 0, 0)))
    return pl.pallas_call(
        functools.partial(_page_mean_body, ppb=ppb), grid_spec=grid_spec,
        out_shape=jax.ShapeDtypeStruct((bsz, nblk, H, hd), F32),
        compiler_params=_cparams(("parallel", "arbitrary")), name="paged_block_means",
    )(page_table, *([cache_k] * pps))


def _decode_select_body(q_ref, km_ref, idx_ref, *, nblk):
    q = q_ref[0]
    lane = lax.broadcasted_iota(jnp.int32, (q.shape[0], LANES), 1)
    gate = jnp.full((q.shape[0], LANES), NEG, F32)
    for n in range(nblk):
        gn = jnp.sum(q * km_ref[0, n], axis=1, keepdims=True)
        gate = jnp.where(lane == n, gn, gate)
    out = jnp.zeros((q.shape[0], LANES), jnp.int32)
    for j in range(MOBA_TOPK):
        mx = jnp.max(gate, axis=1, keepdims=True)
        am = jnp.min(jnp.where(gate == mx, lane, LANES), axis=1, keepdims=True)
        out = jnp.where(lane == j, am, out)
        gate = jnp.where(lane == am, -jnp.inf, gate)
    idx_ref[0] = out


def decode_select(q, kmean):
    bsz, H, hd = q.shape
    nblk = kmean.shape[1]
    assert MOBA_TOPK <= nblk <= LANES
    return pl.pallas_call(
        functools.partial(_decode_select_body, nblk=nblk), grid=(bsz,),
        in_specs=[pl.BlockSpec((1, H, hd), lambda b: (b, 0, 0)),
                  pl.BlockSpec((1, nblk, H, hd), lambda b: (b, 0, 0, 0))],
        out_specs=pl.BlockSpec((1, H, LANES), lambda b: (b, 0, 0)),
        out_shape=jax.ShapeDtypeStruct((bsz, H, LANES), jnp.int32),
        compiler_params=_cparams(("parallel",)), name="decode_select")(q, kmean)


def _decode_attend_body(idx_ref, pt_ref, q_ref, ks_ref, vs_ref, sl_ref, k_hbm, v_hbm, o_ref,
                        kbuf, vbuf, sems, *, n_heads, n_pages, page, past_len):
    s = pl.program_id(0)
    n_steps = pl.num_programs(0)
    ppb = MOBA_BLOCK // page
    n_slab = MOBA_TOPK * ppb

    def slab_copies(step, slot):
        b, h = step // n_heads, step % n_heads
        cps = []
        for t in range(n_slab):
            blk_id = idx_ref[step * MOBA_TOPK + t // ppb]
            pg = pt_ref[b * n_pages + blk_id * ppb + t % ppb]
            cps.append(pltpu.make_async_copy(k_hbm.at[pg, :, h, :], kbuf.at[slot, t], sems.at[slot, 0]))
            cps.append(pltpu.make_async_copy(v_hbm.at[pg, :, h, :], vbuf.at[slot, t], sems.at[slot, 1]))
        return cps

    @pl.when(s == 0)
    def _():
        for cp in slab_copies(s, 0):
            cp.start()

    @pl.when(s + 1 < n_steps)
    def _():
        for cp in slab_copies(s + 1, (s + 1) % 2):
            cp.start()

    slot = s % 2
    for cp in slab_copies(s, slot):
        cp.wait()

    q = q_ref[0]
    slope = sl_ref[0][:, 0:1]
    sub = lax.broadcasted_iota(jnp.int32, (page, 1), 0)
    s_self = jnp.sum(q * ks_ref[0], axis=1, keepdims=True)
    scores = []
    for t in range(n_slab):
        blk_id = idx_ref[s * MOBA_TOPK + t // ppb]
        kpos = blk_id * MOBA_BLOCK + (t % ppb) * page + sub
        dist = (past_len - kpos).astype(F32)
        scores.append(jnp.sum(kbuf[slot, t] * q, axis=1, keepdims=True) - slope * dist)
    mx = s_self
    for sc in scores:
        mx = jnp.maximum(mx, jnp.max(sc, axis=0, keepdims=True))
    p_self = jnp.exp(s_self - mx)
    den = p_self
    acc = p_self * vs_ref[0]
    for t, sc in enumerate(scores):
        p = jnp.exp(sc - mx)
        den = den + jnp.sum(p, axis=0, keepdims=True)
        acc = acc + jnp.sum(p * vbuf[slot, t], axis=0, keepdims=True)
    o_ref[0] = acc / den


def decode_attend(q, k_self, v_self, idx, cache_k, cache_v, page_table, slopes, *, past_len, name):
    n, _, hd = q.shape
    _, page, H, _ = cache_k.shape
    bsz, n_pages = page_table.shape
    n_slab = MOBA_TOPK * (MOBA_BLOCK // page)
    row = pl.BlockSpec((1, 1, hd), lambda s, idx_r, pt_r: (s, 0, 0))
    grid_spec = pltpu.PrefetchScalarGridSpec(
        num_scalar_prefetch=2, grid=(n,),
        in_specs=[row, row, row,
                  pl.BlockSpec((1, 1, LANES), lambda s, idx_r, pt_r: (s % H, 0, 0)),
                  pl.BlockSpec(memory_space=pl.ANY), pl.BlockSpec(memory_space=pl.ANY)],
        out_specs=row,
        scratch_shapes=[pltpu.VMEM((2, n_slab, page, hd), F32), pltpu.VMEM((2, n_slab, page, hd), F32),
                        pltpu.SemaphoreType.DMA((2, 2))])
    body = functools.partial(_decode_attend_body, n_heads=H, n_pages=n_pages, page=page, past_len=past_len)
    return pl.pallas_call(
        body, grid_spec=grid_spec, out_shape=jax.ShapeDtypeStruct((n, 1, hd), F32),
        compiler_params=_cparams(("arbitrary",)), name=name,
    )(idx, page_table.reshape(-1), q, k_self, v_self,
      jnp.broadcast_to(slopes.reshape(H, 1, 1), (H, 1, LANES)), cache_k, cache_v)


def kernel(x_prompt, x_sample, state_mlstm_C, state_mlstm_n, state_mlstm_m, state_conv, cache_k, cache_v, page_table, a_norm_g, a_w_in, a_b_if, a_head_g, a_w_out, kv_norm_g, w_kv, k_norm_g, b_norm_g, b_w_q, q_norm_g, b_w_o, f_norm_g, f_w_up, f_conv_w, f_conv_b, f_w_down):
    bsz, seq, d = x_prompt.shape
    dbsz, dseq, _ = x_sample.shape
    assert dseq == 1
    n_a, depth = a_w_in.shape[0], f_w_up.shape[0]
    n_b = depth - n_a
    dff = f_conv_w.shape[2]
    H = A_HEADS
    dk, dv = state_mlstm_C.shape[3], state_mlstm_C.shape[4]
    hd = d // B_HEADS
    past_len = page_table.shape[1] * cache_k.shape[1]
    s_main = 2 * H * dk + 2 * H * dv
    slopes = jnp.exp2(-8.0 * jnp.arange(1, B_HEADS + 1, dtype=F32) / B_HEADS)

    xp = x_prompt.reshape(bsz * seq, d)
    xs = x_sample.reshape(dbsz, d)
    Cp, Np, Mp, Cs, Ns, Ms, conv_p, conv_s = [], [], [], [], [], [], [], []

    dffp = -(-dff // FFN_TILE) * FFN_TILE
    pad = dffp - dff

    def conv_ffn(layer, xp, xs):
        w_u = jnp.pad(f_w_up[layer, :, :dff].astype(BF16), ((0, 0), (0, pad)))
        w_g = jnp.pad(f_w_up[layer, :, dff:].astype(BF16), ((0, 0), (0, pad)))
        w_down = jnp.pad(f_w_down[layer].astype(BF16), ((0, pad), (0, 0)))
        cw = jnp.pad(f_conv_w[layer], ((0, 0), (0, pad)))
        cb = jnp.pad(f_conv_b[layer], (0, pad))
        act_p, gt_p = ffn_up(xp, f_norm_g[layer], w_u, w_g, cw, cb, tile=FFN_TILE, seq_len=seq,
                             name=f"ffn_up_p{layer}")
        xp = matmul_res(act_p, w_down, xp, name=f"ffn_down_p{layer}")
        conv_p.append(gt_p.reshape(bsz, -1, SUBLANES, dffp)[:, -1, SUBLANES - (CONV_W - 1):, :dff])
        buf = state_conv[layer]
        act_s, g_s = ffn_up(xs, f_norm_g[layer], f_w_up, f_w_up, f_conv_w[layer], f_conv_b[layer], tile=LANES,
                            g_col0=dff // LANES, w_layer=layer, prev=(buf[:, 0, :], buf[:, 1, :]),
                            name=f"ffn_up_s{layer}")
        xs = matmul_res(act_s, f_w_down, xs, w_layer=layer, name=f"ffn_down_s{layer}")
        conv_s.append(jnp.stack([buf[:, 1, :], g_s], axis=1))
        return xp, xs

    for a in range(n_a):
        w_main = a_w_in[a][:, :s_main].astype(BF16)
        w_gate = jnp.zeros((d, LANES), BF16).at[:, :2 * H].set(a_w_in[a][:, s_main:].astype(BF16))
        w_out = a_w_out[a].astype(BF16)
        proj_p, gates_p = norm_matmul(xp, a_norm_g[a], w_main, w_side=w_gate, name=f"mlstm_in_p{a}")
        h_p, c1, n1, m1 = mlstm(
            proj_p.reshape(bsz, seq, s_main), gates_p.reshape(bsz, seq, LANES), a_b_if[a], a_head_g[a],
            jnp.zeros((bsz, H, dk, dv), F32), jnp.zeros((bsz, H, dk), F32), jnp.zeros((bsz, H), F32),
            n_valid=A_CHUNK, name=f"mlstm_p{a}")
        xp = matmul_res(h_p.reshape(bsz * seq, H * dv), w_out, xp, name=f"mlstm_out_p{a}")
        w_gate_s = jnp.zeros((d, LANES), F32).at[:, :2 * H].set(a_w_in[a][:, s_main:])
        proj_s, gates_s = norm_matmul(xs, a_norm_g[a], a_w_in, w_layer=a, n_cols=s_main, w_side=w_gate_s,
                                      name=f"mlstm_in_s{a}")
        proj_s = jnp.zeros((dbsz, A_CHUNK, s_main), F32).at[:, 0, :].set(proj_s)
        gates_s = jnp.zeros((dbsz, A_CHUNK, LANES), F32).at[:, 0, :].set(gates_s)
        h_s, c2, n2, m2 = mlstm(proj_s, gates_s, a_b_if[a], a_head_g[a],
                                state_mlstm_C[a], state_mlstm_n[a], state_mlstm_m[a],
                                n_valid=1, precise=True, name=f"mlstm_s{a}")
        xs = matmul_res(h_s[:, 0, :], a_w_out, xs, w_layer=a, name=f"mlstm_out_s{a}")
        Cp.append(c1), Np.append(n1), Mp.append(m1.reshape(bsz, H))
        Cs.append(c2), Ns.append(n2), Ms.append(m2.reshape(dbsz, H))
        xp, xs = conv_ffn(a, xp, xs)

    w_kv_b = w_kv.astype(BF16)
    kv_p, k_p4, v_p4 = norm_matmul(xp, kv_norm_g, w_kv_b, head_g=k_norm_g, n_norm_cols=d, heads_out=True,
                                   name="shared_kv_p")
    kv_s = norm_matmul(xs, kv_norm_g, w_kv, head_g=k_norm_g, n_norm_cols=d, name="shared_kv_s")
    k_s, v_s = kv_s[:, :d], kv_s[:, d:]
    kmean_s = paged_block_means(cache_k, page_table)
    kv_p3 = kv_p.reshape(bsz, seq, 2 * d)

    for bl in range(n_b):
        w_q = b_w_q[bl].astype(BF16)
        w_o = b_w_o[bl].astype(BF16)
        q_p = norm_matmul(xp, b_norm_g[bl], w_q, head_g=q_norm_g[bl], n_norm_cols=d,
                          head_scale=hd ** -0.5, name=f"moba_q_p{bl}")
        o_p = moba_seq(q_p.reshape(bsz, seq, d), kv_p3, slopes, name=f"moba_attn_p{bl}")
        xp = matmul_res(o_p.reshape(bsz * seq, d), w_o, xp, name=f"moba_out_p{bl}")
        q_s = norm_matmul(xs, b_norm_g[bl], b_w_q, w_layer=bl, head_g=q_norm_g[bl], n_norm_cols=d,
                          head_scale=hd ** -0.5, name=f"moba_q_s{bl}")
        idx = decode_select(q_s.reshape(dbsz, B_HEADS, hd), kmean_s)[:, :, :MOBA_TOPK].reshape(-1)
        o_s = decode_attend(q_s.reshape(dbsz * B_HEADS, 1, hd), k_s.reshape(dbsz * B_HEADS, 1, hd),
                            v_s.reshape(dbsz * B_HEADS, 1, hd), idx, cache_k, cache_v, page_table,
                            slopes, past_len=past_len, name=f"decode_attend{bl}")
        xs = matmul_res(o_s.reshape(dbsz, d), b_w_o, xs, w_layer=bl, name=f"moba_out_s{bl}")
        xp, xs = conv_ffn(n_a + bl, xp, xs)

    return (xp.reshape(bsz, seq, d), xs.reshape(dbsz, 1, d),
            jnp.stack(Cp), jnp.stack(Np), jnp.stack(Mp),
            jnp.stack(Cs), jnp.stack(Ns), jnp.stack(Ms),
            jnp.stack(conv_p), jnp.stack(conv_s),
            k_p4.reshape(bsz, seq, B_HEADS, hd), v_p4.reshape(bsz, seq, B_HEADS, hd),
            k_s.reshape(dbsz, 1, B_HEADS, hd), v_s.reshape(dbsz, 1, B_HEADS, hd))
```
